```python
import math
import jax, jax.numpy as jnp
from jax import lax
import numpy as np

D_MODEL = 1024
BATCH = 8
SEQ = 8192
DEPTH = 2

HEAD_DIM = 64
ROT_DIM = HEAD_DIM // 4
ROPE_THETA = 500000.0
DIFF_HEADS = 4
DIFF_VDIM = 2 * HEAD_DIM
DIFF_QK_WIDTH = DIFF_HEADS * 2 * HEAD_DIM
DIFF_V_WIDTH = DIFF_HEADS * DIFF_VDIM
DIL_HEADS = 8
DIL_WIDTH = DIL_HEADS * HEAD_DIM
DIL_PATTERNS = ((128, 1), (512, 4), (2048, 16))
RNN_WIDTH = D_MODEL
LRU_BLOCKS = 16
LRU_BLOCK = RNN_WIDTH // LRU_BLOCKS
LRU_C = 8.0
CONV_WIDTH = 4
N_BRANCHES = 3
Q_BLOCK = 128
MEM_TOKENS = 256
XATTN_HEADS = 4
XATTN_HEAD_DIM = D_MODEL // XATTN_HEADS
FFN_HIDDEN = -(-8 * D_MODEL // (3 * 256)) * 256
DEEPNORM_ALPHA = (2 * DEPTH) ** 0.25
DEEPNORM_BETA = (8 * DEPTH) ** -0.25
LN_EPS = 1e-5
IN_SPLITS = (DIFF_QK_WIDTH, DIFF_QK_WIDTH, DIFF_V_WIDTH, RNN_WIDTH, RNN_WIDTH,
             DIL_WIDTH, DIL_WIDTH, DIL_WIDTH, N_BRANCHES * D_MODEL)
IN_WIDTH = sum(IN_SPLITS)

kernel_name = "hybrid_diffattn_rglru_dilated_deepnorm"


def layer_norm(x, g, b):
    xf = x.astype(jnp.float32)
    mu = jnp.mean(xf, axis=-1, keepdims=True)
    var = jnp.mean(jnp.square(xf - mu), axis=-1, keepdims=True)
    y = (xf - mu) * lax.rsqrt(var + LN_EPS)
    return (y * g.astype(jnp.float32) + b.astype(jnp.float32)).astype(x.dtype)


def rms_norm(x, g):
    xf = x.astype(jnp.float32)
    y = xf * lax.rsqrt(jnp.mean(xf * xf, axis=-1, keepdims=True) + LN_EPS)
    return (y * g.astype(jnp.float32)).astype(x.dtype)


def partial_rotary(x, pos):
    half = ROT_DIM // 2
    inv_freq = jnp.power(ROPE_THETA, -2.0 * jnp.arange(half, dtype=jnp.float32) / ROT_DIM)
    ang = pos.astype(jnp.float32)[:, None] * inv_freq[None, :]
    shape = (1, ang.shape[0]) + (1,) * (x.ndim - 3) + (half,)
    cos = jnp.cos(ang).reshape(shape)
    sin = jnp.sin(ang).reshape(shape)
    xf = x.astype(jnp.float32)
    x1 = xf[..., :half]
    x2 = xf[..., half:ROT_DIM]
    out = jnp.concatenate([x1 * cos - x2 * sin, x2 * cos + x1 * sin, xf[..., ROT_DIM:]], axis=-1)
    return out.astype(x.dtype)


def diff_attention(q, k, v, lam, subln_g, lambda_init):
    B, S = q.shape[0], q.shape[1]
    nb = S // Q_BLOCK
    scale = HEAD_DIM ** -0.5
    kpos = jnp.arange(S)
    qb = q.reshape(B, nb, Q_BLOCK, DIFF_HEADS, 2, HEAD_DIM).transpose(1, 0, 2, 3, 4, 5)
    starts = jnp.arange(nb) * Q_BLOCK

    def block(args):
        qblk, start = args
        s = jnp.einsum('bqhcd,bkhcd->bhcqk', qblk, k).astype(jnp.float32) * scale
        causal = (start + jnp.arange(Q_BLOCK))[:, None] >= kpos[None, :]
        p = jax.nn.softmax(jnp.where(causal, s, -jnp.inf), axis=-1)
        w = p[:, :, 0] - lam * p[:, :, 1]
        return jnp.einsum('bhqk,bkhe->bqhe', w.astype(v.dtype), v)

    o = lax.map(block, (qb, starts))
    o = o.transpose(1, 0, 2, 3, 4).reshape(B, S, DIFF_HEADS, DIFF_VDIM)
    o = rms_norm(o, subln_g) * (1.0 - lambda_init)
    return o.reshape(B, S, DIFF_V_WIDTH)


def dilated_band_attention(q, k, v, dil, steps):
    B, S, H, Dh = q.shape
    pad = (-S) % (dil * steps)
    Sp = S + pad
    M = Sp // dil
    nb = M // steps

    def to_blocks(t):
        t = jnp.pad(t, ((0, 0), (0, pad), (0, 0), (0, 0)))
        t = t.reshape(B, M, dil, H, Dh).transpose(0, 2, 1, 3, 4)
        return t.reshape(B, dil, nb, steps, H, Dh)

    qb, kb, vb = to_blocks(q), to_blocks(k), to_blocks(v)
    zpad = ((0, 0), (0, 0), (1, 0), (0, 0), (0, 0), (0, 0))
    kk = jnp.concatenate([jnp.pad(kb, zpad)[:, :, :-1], kb], axis=3)
    vv = jnp.concatenate([jnp.pad(vb, zpad)[:, :, :-1], vb], axis=3)
    s = jnp.einsum('bpnqhd,bpnkhd->bpnhqk', qb, kk).astype(jnp.float32) * (Dh ** -0.5)
    i = jnp.arange(steps)[:, None]
    j = jnp.arange(2 * steps)[None, :]
    dist = steps + i - j
    band = (dist >= 0) & (dist <= steps)
    valid = band[None] & ((jnp.arange(nb)[:, None, None] > 0) | (j >= steps)[None])
    s = jnp.where(valid[None, None, :, None], s, -jnp.inf)
    lse = jax.nn.logsumexp(s, axis=-1)
    p = jnp.exp(s - lse[..., None])
    o = jnp.einsum('bpnhqk,bpnkhd->bpnqhd', p.astype(v.dtype), vv)
    o = o.reshape(B, dil, M, H, Dh).transpose(0, 2, 1, 3, 4).reshape(B, Sp, H, Dh)[:, :S]
    lse = lse.transpose(0, 1, 2, 4, 3).reshape(B, dil, M, H).transpose(0, 2, 1, 3).reshape(B, Sp, H)[:, :S]
    return o, lse


def dilated_attention(q, k, v):
    outs, lses = [], []
    for window, dil in DIL_PATTERNS:
        o, lse = dilated_band_attention(q, k, v, dil, window // dil)
        outs.append(o.astype(jnp.float32))
        lses.append(lse)
    wts = jax.nn.softmax(jnp.stack(lses, axis=0), axis=0)
    o = jnp.sum(wts[..., None] * jnp.stack(outs, axis=0), axis=0)
    B, S = q.shape[0], q.shape[1]
    return o.reshape(B, S, DIL_WIDTH).astype(q.dtype)


def rg_lru_branch(xr, gate, conv_w, conv_b, gate_w, gate_b, lru_lambda):
    B, S, _ = xr.shape
    xp = jnp.pad(xr, ((0, 0), (CONV_WIDTH - 1, 0), (0, 0)))
    xc = conv_b
    for tap in range(CONV_WIDTH):
        xc = xc + xp[:, tap:tap + S] * conv_w[tap]
    xb = xc.reshape(B, S, LRU_BLOCKS, LRU_BLOCK)
    gates = jnp.einsum('bsnc,gncd->gbsnd', xb, gate_w).reshape(2, B, S, RNN_WIDTH) + gate_b[:, None, None, :]
    r = jax.nn.sigmoid(gates[0].astype(jnp.float32))
    i = jax.nn.sigmoid(gates[1].astype(jnp.float32))
    log_a = -LRU_C * r * jax.nn.softplus(-lru_lambda.astype(jnp.float32))
    a = jnp.exp(log_a)
    u = jnp.sqrt(-jnp.expm1(2.0 * log_a)) * (i * xc.astype(jnp.float32))

    def step(h, inp):
        a_t, u_t = inp
        h = a_t * h + u_t
        return h, h

    _, hs = lax.scan(step, jnp.zeros((B, RNN_WIDTH), jnp.float32),
                     (a.transpose(1, 0, 2), u.transpose(1, 0, 2)))
    h = hs.transpose(1, 0, 2).astype(xr.dtype)
    return h * jax.nn.gelu(gate)


def hybrid_mixer(x, layer, w_in, lam_qk, diff_subln, conv_w, conv_b, gate_w, gate_b, lru_lambda,
                 w_br_a, w_br_b, w_br_c, w_out):
    B, S, _ = x.shape
    pos = jnp.arange(S)
    u = x @ w_in
    offsets = np.cumsum(IN_SPLITS)[:-1].tolist()
    aq, ak, av, rx, rg, cq, ck, cv, gl = jnp.split(u, offsets, axis=-1)
    lambda_init = 0.8 - 0.6 * math.exp(-0.3 * layer)
    lq = lam_qk.astype(jnp.float32)
    lam = jnp.exp(jnp.sum(lq[0] * lq[1])) - jnp.exp(jnp.sum(lq[2] * lq[3])) + lambda_init
    qa = partial_rotary(aq.reshape(B, S, DIFF_HEADS, 2, HEAD_DIM), pos)
    ka = partial_rotary(ak.reshape(B, S, DIFF_HEADS, 2, HEAD_DIM), pos)
    o_a = diff_attention(qa, ka, av.reshape(B, S, DIFF_HEADS, DIFF_VDIM), lam, diff_subln, lambda_init)
    o_b = rg_lru_branch(rx, rg, conv_w, conv_b, gate_w, gate_b, lru_lambda)
    qc = partial_rotary(cq.reshape(B, S, DIL_HEADS, HEAD_DIM), pos)
    kc = partial_rotary(ck.reshape(B, S, DIL_HEADS, HEAD_DIM), pos)
    o_c = dilated_attention(qc, kc, cv.reshape(B, S, DIL_HEADS, HEAD_DIM))
    g = jax.nn.sigmoid(gl.reshape(B, S, N_BRANCHES, D_MODEL))
    merged = g[:, :, 0] * (o_a @ w_br_a) + g[:, :, 1] * (o_b @ w_br_b) + g[:, :, 2] * (o_c @ w_br_c)
    return merged @ w_out


def memory_cross_attention(x, mem, wq, wkv, wo):
    B, S, _ = x.shape
    q = (x @ wq).reshape(B, S, XATTN_HEADS, XATTN_HEAD_DIM)
    k, v = jnp.split(mem @ wkv, 2, axis=-1)
    k = k.reshape(B, -1, XATTN_HEADS, XATTN_HEAD_DIM)
    v = v.reshape(B, -1, XATTN_HEADS, XATTN_HEAD_DIM)
    s = jnp.einsum('bqhd,bmhd->bhqm', q, k).astype(jnp.float32) * (XATTN_HEAD_DIM ** -0.5)
    p = jax.nn.softmax(s, axis=-1)
    o = jnp.einsum('bhqm,bmhd->bqhd', p.astype(v.dtype), v).reshape(B, S, D_MODEL)
    return o @ wo


def swiglu(x, w_up, w_down):
    gpart, upart = jnp.split(x @ w_up, 2, axis=-1)
    return (jax.nn.silu(gpart) * upart) @ w_down


def setup_inputs(seed: int = 0) -> dict:
    key = jax.random.key(seed)
    ks = iter(jax.random.split(key, 32))
    nrm = lambda shape, scale: jax.random.normal(next(ks), shape, jnp.float32) * scale
    L, D = DEPTH, D_MODEL
    u = jax.random.uniform(next(ks), (L, RNN_WIDTH), jnp.float32, minval=0.9, maxval=0.999)
    p = u ** (1.0 / LRU_C)
    lru_lambda = jnp.log(p) - jnp.log1p(-p)
    return {
        "x": nrm((BATCH, SEQ, D), 1.0),
        "mem": nrm((BATCH, MEM_TOKENS, D), 1.0),
        "w_in": nrm((L, D, IN_WIDTH), D ** -0.5),
        "lam_qk": nrm((L, 4, HEAD_DIM), 0.1),
        "diff_subln": 1.0 + nrm((L, DIFF_VDIM), 0.05),
        "conv_w": nrm((L, CONV_WIDTH, RNN_WIDTH), CONV_WIDTH ** -0.5),
        "conv_b": nrm((L, RNN_WIDTH), 0.01),
        "gate_w": nrm((L, 2, LRU_BLOCKS, LRU_BLOCK, LRU_BLOCK), LRU_BLOCK ** -0.5),
        "gate_b": nrm((L, 2, RNN_WIDTH), 0.01),
        "lru_lambda": lru_lambda,
        "w_br_a": nrm((L, DIFF_V_WIDTH, D), DIFF_V_WIDTH ** -0.5),
        "w_br_b": nrm((L, RNN_WIDTH, D), RNN_WIDTH ** -0.5),
        "w_br_c": nrm((L, DIL_WIDTH, D), DIL_WIDTH ** -0.5),
        "w_out": nrm((L, D, D), D ** -0.5 * DEEPNORM_BETA),
        "ln1_g": 1.0 + nrm((L, D), 0.02),
        "ln1_b": nrm((L, D), 0.02),
        "xq": nrm((L, D, D), D ** -0.5),
        "xkv": nrm((L, D, 2 * D), D ** -0.5),
        "xo": nrm((L, D, D), D ** -0.5 * DEEPNORM_BETA),
        "ln2_g": 1.0 + nrm((L, D), 0.02),
        "ln2_b": nrm((L, D), 0.02),
        "w_up": nrm((L, D, 2 * FFN_HIDDEN), D ** -0.5),
        "w_down": nrm((L, FFN_HIDDEN, D), FFN_HIDDEN ** -0.5 * DEEPNORM_BETA),
        "ln3_g": 1.0 + nrm((L, D), 0.02),
        "ln3_b": nrm((L, D), 0.02),
    }


def reference(x, mem, w_in, lam_qk, diff_subln, conv_w, conv_b, gate_w, gate_b, lru_lambda,
              w_br_a, w_br_b, w_br_c, w_out, ln1_g, ln1_b, xq, xkv, xo, ln2_g, ln2_b,
              w_up, w_down, ln3_g, ln3_b):
    for l in range(DEPTH):
        h = hybrid_mixer(x, l, w_in[l], lam_qk[l], diff_subln[l], conv_w[l], conv_b[l], gate_w[l],
                         gate_b[l], lru_lambda[l], w_br_a[l], w_br_b[l], w_br_c[l], w_out[l])
        x = layer_norm(DEEPNORM_ALPHA * x + h, ln1_g[l], ln1_b[l])
        x = layer_norm(DEEPNORM_ALPHA * x + memory_cross_attention(x, mem, xq[l], xkv[l], xo[l]),
                       ln2_g[l], ln2_b[l])
        x = layer_norm(DEEPNORM_ALPHA * x + swiglu(x, w_up[l], w_down[l]), ln3_g[l], ln3_b[l])
    return x
```

```python
import functools
import math

import jax
import jax.numpy as jnp
from jax import lax
from jax.experimental import pallas as pl
from jax.experimental.pallas import tpu as pltpu

D_MODEL = 1024
DEPTH = 2
HEAD_DIM = 64
ROT_DIM = HEAD_DIM // 4
ROPE_THETA = 500000.0
DIFF_HEADS = 4
DIFF_WIDTH = 512
DIL_HEADS = 8
DIL_WIDTH = 512
DIL_STEPS = 128
DILATIONS = (1, 4, 16)
RNN_WIDTH = D_MODEL
LRU_BLOCK = 64
LRU_C = 8.0
CONV_WIDTH = 4
MEM_TOKENS = 256
XATTN_HEADS = 4
XATTN_HEAD_DIM = D_MODEL // XATTN_HEADS
FFN_HIDDEN = 2816
DEEPNORM_ALPHA = (2 * DEPTH) ** 0.25
LN_EPS = 1e-5

COL_QK_A = 0
COL_V_A = 1024
COL_RNN = 1536
COL_QK_C = 3584
COL_V_C = 4608
COL_GATE = 5120

LANES = 128
GATE_TILE = 256
V7X_VMEM_BYTES = 64 * 1024 * 1024
VMEM_LIMIT = V7X_VMEM_BYTES * 3 // 4

_BF16 = jnp.bfloat16
_F32 = jnp.float32
_NT = (((1,), (1,)), ((), ()))


def _params(*semantics):
    return pltpu.CompilerParams(dimension_semantics=semantics, vmem_limit_bytes=VMEM_LIMIT)


def _layer_norm(z, g, b):
    mu = jnp.mean(z, axis=-1, keepdims=True)
    d = z - mu
    var = jnp.mean(d * d, axis=-1, keepdims=True)
    return d * lax.rsqrt(var + LN_EPS) * g + b


def _proj_kernel(x_ref, w_ref, *rest, rotary):
    acc = jnp.dot(x_ref[...], w_ref[...], preferred_element_type=_F32)
    if not rotary:
        (o_ref,) = rest
        o_ref[...] = acc.astype(o_ref.dtype)
        return
    cos_ref, sdn_ref, sup_ref, o_ref = rest
    c, sdn, sup = cos_ref[...], sdn_ref[...], sup_ref[...]
    half = ROT_DIM // 2
    for j in range(acc.shape[1] // LANES):
        blk = acc[:, j * LANES:(j + 1) * LANES]
        dn = pltpu.roll(blk, LANES - half, axis=1)
        up = pltpu.roll(blk, half, axis=1)
        o_ref[:, j * LANES:(j + 1) * LANES] = (blk * c + dn * sdn + up * sup).astype(o_ref.dtype)


def _rotary_tables(seq):
    half = ROT_DIM // 2
    inv_freq = jnp.power(ROPE_THETA, -2.0 * jnp.arange(half, dtype=_F32) / ROT_DIM)
    ang = jnp.arange(seq).astype(_F32)[:, None] * inv_freq[None, :]
    cos, sin = jnp.cos(ang), jnp.sin(ang)
    ones = jnp.ones((seq, HEAD_DIM - ROT_DIM), _F32)
    zeros = jnp.zeros((seq, HEAD_DIM - ROT_DIM), _F32)
    zh = jnp.zeros((seq, half), _F32)
    c = jnp.concatenate([cos, cos, ones], axis=1)
    sdn = jnp.concatenate([-sin, zh, zeros], axis=1)
    sup = jnp.concatenate([zh, sin, zeros], axis=1)
    reps = LANES // HEAD_DIM
    return tuple(jnp.tile(t, (1, reps)) for t in (c, sdn, sup))


def _project(name, x2, w, col0, width, out_dtype, seq, tables=None, tm=1024, tn=512):
    n, k = x2.shape
    tm = min(tm, seq)
    tn = min(tn, width)
    assert n % tm == 0 and width % tn == 0 and col0 % tn == 0 and seq % tm == 0
    grid = (width // tn, n // tm)
    in_specs = [pl.BlockSpec((tm, k), lambda j, i: (i, 0)),
                pl.BlockSpec((k, tn), lambda j, i: (0, col0 // tn + j))]
    args = [x2, w]
    if tables is not None:
        tspec = pl.BlockSpec((tm, LANES), lambda j, i: (i % (seq // tm), 0))
        in_specs += [tspec] * 3
        args += list(tables)
    return pl.pallas_call(
        functools.partial(_proj_kernel, rotary=tables is not None),
        out_shape=jax.ShapeDtypeStruct((n, width), out_dtype),
        grid=grid,
        in_specs=in_specs,
        out_specs=pl.BlockSpec((tm, tn), lambda j, i: (i, j)),
        compiler_params=_params("parallel", "parallel"),
        name=name,
    )(*args)


def _diff_attn_kernel(lam_ref, g_ref, q_ref, k_ref, v_ref, o_ref,
                      m1, l1, a1, m2, l2, a2, *, lambda_init):
    qi = pl.program_id(2)
    ki = pl.program_id(3)
    tq = q_ref.shape[1]
    tk = k_ref.shape[1]
    stats = ((m1, l1, a1), (m2, l2, a2))

    @pl.when(ki == 0)
    def _init():
        for m_s, l_s, a_s in stats:
            m_s[...] = jnp.full_like(m_s, -jnp.inf)
            l_s[...] = jnp.zeros_like(l_s)
            a_s[...] = jnp.zeros_like(a_s)

    def step(masked):
        q = q_ref[0] * (HEAD_DIM ** -0.5)
        k = k_ref[0]
        v = v_ref[0]
        for c, (m_s, l_s, a_s) in enumerate(stats):
            s = lax.dot_general(q[:, c * HEAD_DIM:(c + 1) * HEAD_DIM],
                                k[:, c * HEAD_DIM:(c + 1) * HEAD_DIM], _NT,
                                preferred_element_type=_F32)
            if masked:
                rows = lax.broadcasted_iota(jnp.int32, (tq, tk), 0)
                cols = lax.broadcasted_iota(jnp.int32, (tq, tk), 1)
                s = jnp.where(rows >= cols, s, -jnp.inf)
            m_prev = m_s[...]
            m_new = jnp.maximum(m_prev, jnp.max(s, axis=1, keepdims=True))
            alpha = jnp.exp(m_prev - m_new)
            p = jnp.exp(s - m_new)
            l_s[...] = alpha * l_s[...] + jnp.sum(p, axis=1, keepdims=True)
            a_s[...] = alpha * a_s[...] + jnp.dot(p.astype(_BF16), v, preferred_element_type=_F32)
            m_s[...] = m_new

    @pl.when(ki < qi)
    def _full():
        step(False)

    @pl.when(ki == qi)
    def _diag():
        step(True)
        lq = lam_ref[...]
        lam = (jnp.exp(jnp.sum(lq[0:1] * lq[1:2], axis=1, keepdims=True))
               - jnp.exp(jnp.sum(lq[2:3] * lq[3:4], axis=1, keepdims=True)) + lambda_init)
        o = a1[...] / l1[...] - lam * (a2[...] / l2[...])
        y = o * lax.rsqrt(jnp.mean(o * o, axis=1, keepdims=True) + LN_EPS) * g_ref[...]
        o_ref[0] = (y * (1.0 - lambda_init)).astype(o_ref.dtype)


def _diff_attention(qk, v, lam_qk, subln, lambda_init, tile=512):
    b, s, _ = qk.shape
    t = min(tile, s)
    assert s % t == 0
    nt = s // t
    width = 2 * HEAD_DIM
    k_off = DIFF_WIDTH // width
    return pl.pallas_call(
        functools.partial(_diff_attn_kernel, lambda_init=lambda_init),
        out_shape=jax.ShapeDtypeStruct((b, s, DIFF_WIDTH), _BF16),
        grid=(b, DIFF_HEADS, nt, nt),
        in_specs=[
            pl.BlockSpec((4, HEAD_DIM), lambda bi, h, qi, ki: (0, 0)),
            pl.BlockSpec((1, width), lambda bi, h, qi, ki: (0, 0)),
            pl.BlockSpec((1, t, width), lambda bi, h, qi, ki: (bi, qi, h)),
            pl.BlockSpec((1, t, width), lambda bi, h, qi, ki: (bi, jnp.minimum(ki, qi), k_off + h)),
            pl.BlockSpec((1, t, width), lambda bi, h, qi, ki: (bi, jnp.minimum(ki, qi), h)),
        ],
        out_specs=pl.BlockSpec((1, t, width), lambda bi, h, qi, ki: (bi, qi, h)),
        scratch_shapes=[pltpu.VMEM((t, 1), _F32), pltpu.VMEM((t, 1), _F32), pltpu.VMEM((t, width), _F32),
                        pltpu.VMEM((t, 1), _F32), pltpu.VMEM((t, 1), _F32), pltpu.VMEM((t, width), _F32)],
        compiler_params=_params("parallel", "parallel", "parallel", "arbitrary"),
        name="diff_attn",
    )(lam_qk, subln.reshape(1, width), qk, qk, v)


HIST = 8


def _lru_kernel(rxg_ref, cw_ref, cb_ref, wbd_ref, gb_ref, lam_ref, o_ref, xbuf, a_buf, u_buf, h_ref):
    t = o_ref.shape[1]
    w = RNN_WIDTH

    @pl.when(pl.program_id(1) == 0)
    def _init():
        xbuf[0:HIST, :] = jnp.zeros((HIST, w), _F32)
        h_ref[...] = jnp.zeros_like(h_ref)

    xbuf[HIST:HIST + t, :] = rxg_ref[0, :, 0:w]
    xc = cb_ref[...]
    for tap in range(CONV_WIDTH):
        start = HIST - (CONV_WIDTH - 1) + tap
        xc = xc + xbuf[start:start + t, :] * cw_ref[tap:tap + 1, :]
    xbuf[0:HIST, :] = xbuf[t:t + HIST, :]

    xcb = xc.astype(_BF16)
    gates = []
    for g in range(2):
        parts = [jnp.dot(xcb[:, j * GATE_TILE:(j + 1) * GATE_TILE], wbd_ref[g, j],
                         preferred_element_type=_F32) for j in range(w // GATE_TILE)]
        gates.append(jnp.concatenate(parts, axis=1) + gb_ref[g:g + 1, :])
    r = jax.nn.sigmoid(gates[0])
    i = jax.nn.sigmoid(gates[1])
    neg = -lam_ref[...]
    softplus = jnp.maximum(neg, 0.0) + jnp.log1p(jnp.exp(-jnp.abs(neg)))
    log_a = -LRU_C * r * softplus
    a = jnp.exp(log_a)
    a_buf[...] = a
    u_buf[...] = jnp.sqrt(-jnp.tanh(log_a) * (1.0 + a * a)) * (i * xc)

    def body(row, h):
        h = a_buf[pl.ds(row, 1), :] * h + u_buf[pl.ds(row, 1), :]
        u_buf[pl.ds(row, 1), :] = h
        return h

    h_ref[...] = lax.fori_loop(0, t, body, h_ref[...], unroll=8)
    o_ref[0] = (u_buf[...] * jax.nn.gelu(rxg_ref[0, :, w:2 * w])).astype(o_ref.dtype)


def _block_diag_gates(gate_w):
    per = GATE_TILE // LRU_BLOCK
    n_tiles = RNN_WIDTH // GATE_TILE
    gw = gate_w.reshape(2, n_tiles, per, LRU_BLOCK, LRU_BLOCK)
    eye = jnp.eye(per, dtype=gate_w.dtype)
    dense = jnp.einsum("gtpcd,pq->gtpcqd", gw, eye)
    return dense.reshape(2, n_tiles, GATE_TILE, GATE_TILE)


def _rg_lru(rxg, conv_w, conv_b, gate_w, gate_b, lru_lambda, tile=512):
    b, s, _ = rxg.shape
    t = min(tile, s)
    assert s % t == 0
    w = RNN_WIDTH
    full = lambda shape: pl.BlockSpec(shape, lambda bi, ti: (0,) * len(shape))
    return pl.pallas_call(
        _lru_kernel,
        out_shape=jax.ShapeDtypeStruct((b, s, w), _BF16),
        grid=(b, s // t),
        in_specs=[pl.BlockSpec((1, t, 2 * w), lambda bi, ti: (bi, ti, 0)),
                  full((CONV_WIDTH, w)), full((1, w)),
                  full((2, w // GATE_TILE, GATE_TILE, GATE_TILE)), full((2, w)), full((1, w))],
        out_specs=pl.BlockSpec((1, t, w), lambda bi, ti: (bi, ti, 0)),
        scratch_shapes=[pltpu.VMEM((t + HIST, w), _F32), pltpu.VMEM((t, w), _F32),
                        pltpu.VMEM((t, w), _F32), pltpu.VMEM((1, w), _F32)],
        compiler_params=_params("parallel", "arbitrary"),
        name="rg_lru",
    )(rxg, conv_w, conv_b.reshape(1, w), _block_diag_gates(gate_w).astype(_BF16), gate_b,
      lru_lambda.reshape(1, w))


def _dil_kernel(*refs, has_prev, is_last):
    q_ref, kp_ref, kc_ref, vp_ref, vc_ref = refs[:5]
    refs = refs[5:]
    if has_prev:
        oprev_ref, lprev_ref = refs[:2]
        refs = refs[2:]
    o_ref = refs[0]
    l_ref = None if is_last else refs[1]

    n = pl.program_id(2)
    st = DIL_STEPS
    q = q_ref[0] * (HEAD_DIM ** -0.5)
    kk = jnp.concatenate([kp_ref[0], kc_ref[0]], axis=0)
    vv = jnp.concatenate([vp_ref[0], vc_ref[0]], axis=0)
    ii = lax.broadcasted_iota(jnp.int32, (st, 2 * st), 0)
    jj = lax.broadcasted_iota(jnp.int32, (st, 2 * st), 1)
    dist = st + ii - jj
    valid = (dist >= 0) & (dist <= st) & ((n > 0) | (jj >= st))
    if l_ref is not None:
        l_ref[0] = jnp.zeros(l_ref.shape[1:], _F32)
    for h in range(DIL_HEADS):
        sl = slice(h * HEAD_DIM, (h + 1) * HEAD_DIM)
        s = lax.dot_general(q[:, sl], kk[:, sl], _NT, preferred_element_type=_F32)
        s = jnp.where(valid, s, -jnp.inf)
        m = jnp.max(s, axis=1, keepdims=True)
        p = jnp.exp(s - m)
        l = jnp.sum(p, axis=1, keepdims=True)
        o_h = jnp.dot(p.astype(_BF16), vv[:, sl], preferred_element_type=_F32) / l
        lse = m + jnp.log(l)
        if has_prev:
            lp = lprev_ref[0, :, h:h + 1]
            mx = jnp.maximum(lp, lse)
            w_prev = jnp.exp(lp - mx)
            w_cur = jnp.exp(lse - mx)
            den = w_prev + w_cur
            o_h = (w_prev * oprev_ref[0, :, sl] + w_cur * o_h) / den
            lse = mx + jnp.log(den)
        o_ref[0, :, sl] = o_h.astype(o_ref.dtype)
        if l_ref is not None:
            l_ref[0, :, h:h + 1] = lse


def _dilated_pattern(qk, v, dil, prev, is_last):
    b, s, _ = qk.shape
    st = DIL_STEPS
    assert s % (dil * st) == 0
    m = s // dil
    qk_d = qk.reshape(b, m, dil * 2 * DIL_WIDTH)
    v_d = v.reshape(b, m, dil * DIL_WIDTH)
    blk = lambda width, fn: pl.BlockSpec((1, st, width), fn)
    in_specs = [
        blk(DIL_WIDTH, lambda bi, p, n: (bi, n, 2 * p)),
        blk(DIL_WIDTH, lambda bi, p, n: (bi, jnp.maximum(n - 1, 0), 2 * p + 1)),
        blk(DIL_WIDTH, lambda bi, p, n: (bi, n, 2 * p + 1)),
        blk(DIL_WIDTH, lambda bi, p, n: (bi, jnp.maximum(n - 1, 0), p)),
        blk(DIL_WIDTH, lambda bi, p, n: (bi, n, p)),
    ]
    args = [qk_d, qk_d, qk_d, v_d, v_d]
    if prev is not None:
        o_prev, l_prev = prev
        in_specs += [blk(DIL_WIDTH, lambda bi, p, n: (bi, n, p)), blk(LANES, lambda bi, p, n: (bi, n, p))]
        args += [o_prev.reshape(b, m, dil * DIL_WIDTH), l_prev.reshape(b, m, dil * LANES)]
    o_shape = jax.ShapeDtypeStruct((b, m, dil * DIL_WIDTH), _BF16 if is_last else _F32)
    o_spec = blk(DIL_WIDTH, lambda bi, p, n: (bi, n, p))
    if is_last:
        out_shape, out_specs = o_shape, o_spec
    else:
        out_shape = (o_shape, jax.ShapeDtypeStruct((b, m, dil * LANES), _F32))
        out_specs = (o_spec, blk(LANES, lambda bi, p, n: (bi, n, p)))
    out = pl.pallas_call(
        functools.partial(_dil_kernel, has_prev=prev is not None, is_last=is_last),
        out_shape=out_shape,
        grid=(b, dil, m // st),
        in_specs=in_specs,
        out_specs=out_specs,
        compiler_params=_params("parallel", "parallel", "parallel"),
        name=f"dil_attn_d{dil}",
    )(*args)
    if is_last:
        return out.reshape(b, s, DIL_WIDTH)
    return out[0].reshape(b, s, DIL_WIDTH), out[1].reshape(b, s, LANES)


def _dilated_attention(qk, v):
    prev = None
    for idx, dil in enumerate(DILATIONS):
        prev = _dilated_pattern(qk, v, dil, prev, is_last=idx == len(DILATIONS) - 1)
    return prev


def _merge_kernel(oa_ref, ob_ref, oc_ref, gl_ref, x_ref, wa_ref, wb_ref, wc_ref, wo_ref,
                  g_ref, b_ref, y32_ref, y16_ref):
    d = D_MODEL
    merged = None
    for idx, (o_ref, w_ref) in enumerate(((oa_ref, wa_ref), (ob_ref, wb_ref), (oc_ref, wc_ref))):
        branch = jnp.dot(o_ref[...], w_ref[...], preferred_element_type=_F32)
        term = jax.nn.sigmoid(gl_ref[:, idx * d:(idx + 1) * d]) * branch
        merged = term if merged is None else merged + term
    h = jnp.dot(merged.astype(_BF16), wo_ref[...], preferred_element_type=_F32)
    y = _layer_norm(DEEPNORM_ALPHA * x_ref[...] + h, g_ref[...], b_ref[...])
    y32_ref[...] = y
    y16_ref[...] = y.astype(_BF16)


def _merge(o_a, o_b, o_c, gl, x, w_a, w_b, w_c, w_o, g, b, tm=512):
    n, d = x.shape
    tm = min(tm, n)
    assert n % tm == 0
    row = lambda width: pl.BlockSpec((tm, width), lambda i: (i, 0))
    full = lambda shape: pl.BlockSpec(shape, lambda i: (0, 0))
    return pl.pallas_call(
        _merge_kernel,
        out_shape=(jax.ShapeDtypeStruct((n, d), _F32), jax.ShapeDtypeStruct((n, d), _BF16)),
        grid=(n // tm,),
        in_specs=[row(DIFF_WIDTH), row(RNN_WIDTH), row(DIL_WIDTH), row(3 * d), row(d),
                  full((DIFF_WIDTH, d)), full((RNN_WIDTH, d)), full((DIL_WIDTH, d)), full((d, d)),
                  full((1, d)), full((1, d))],
        out_specs=(row(d), row(d)),
        compiler_params=_params("parallel"),
        name="merge_ln1",
    )(o_a, o_b, o_c, gl, x, w_a, w_b, w_c, w_o, g.reshape(1, d), b.reshape(1, d))


def _xattn_kernel(x32_ref, x16_ref, kv_ref, wq_ref, wo_ref, g_ref, b_ref, y32_ref, y16_ref):
    d = D_MODEL
    hd = XATTN_HEAD_DIM
    q = jnp.dot(x16_ref[0], wq_ref[...], preferred_element_type=_F32)
    q = (q * (hd ** -0.5)).astype(_BF16)
    heads = []
    for h in range(XATTN_HEADS):
        k_h = kv_ref[0, :, h * hd:(h + 1) * hd]
        v_h = kv_ref[0, :, d + h * hd:d + (h + 1) * hd]
        s = lax.dot_general(q[:, h * hd:(h + 1) * hd], k_h, _NT, preferred_element_type=_F32)
        p = jnp.exp(s - jnp.max(s, axis=1, keepdims=True))
        l = jnp.sum(p, axis=1, keepdims=True)
        heads.append(jnp.dot(p.astype(_BF16), v_h, preferred_element_type=_F32) / l)
    o = jnp.concatenate(heads, axis=1).astype(_BF16)
    h_out = jnp.dot(o, wo_ref[...], preferred_element_type=_F32)
    y = _layer_norm(DEEPNORM_ALPHA * x32_ref[0] + h_out, g_ref[...], b_ref[...])
    y32_ref[0] = y
    y16_ref[0] = y.astype(_BF16)


def _cross_attention(x32, x16, kv, wq, wo, g, b, tm=512):
    bsz, s, d = x32.shape
    tm = min(tm, s)
    assert s % tm == 0
    row = pl.BlockSpec((1, tm, d), lambda bi, i: (bi, i, 0))
    full = lambda shape: pl.BlockSpec(shape, lambda bi, i: (0, 0))
    return pl.pallas_call(
        _xattn_kernel,
        out_shape=(jax.ShapeDtypeStruct((bsz, s, d), _F32), jax.ShapeDtypeStruct((bsz, s, d), _BF16)),
        grid=(bsz, s // tm),
        in_specs=[row, row, pl.BlockSpec((1, MEM_TOKENS, 2 * d), lambda bi, i: (bi, 0, 0)),
                  full((d, d)), full((d, d)), full((1, d)), full((1, d))],
        out_specs=(row, row),
        compiler_params=_params("parallel", "parallel"),
        name="xattn_ln2",
    )(x32, x16, kv, wq, wo, g.reshape(1, d), b.reshape(1, d))


def _ffn_kernel(x32_ref, x16_ref, wg_ref, wu_ref, wd_ref, g_ref, b_ref, y32_ref, y16_ref, acc_ref):
    c = pl.program_id(1)

    @pl.when(c == 0)
    def _init():
        acc_ref[...] = jnp.zeros_like(acc_ref)

    x = x16_ref[...]
    gate = jnp.dot(x, wg_ref[...], preferred_element_type=_F32)
    up = jnp.dot(x, wu_ref[...], preferred_element_type=_F32)
    hidden = (jax.nn.silu(gate) * up).astype(_BF16)
    acc_ref[...] += jnp.dot(hidden, wd_ref[...], preferred_element_type=_F32)

    @pl.when(c == pl.num_programs(1) - 1)
    def _fin():
        y = _layer_norm(DEEPNORM_ALPHA * x32_ref[...] + acc_ref[...], g_ref[...], b_ref[...])
        y32_ref[...] = y
        y16_ref[...] = y.astype(_BF16)


def _ffn(x32, x16, w_up, w_down, g, b, tm=512, chunks=2):
    n, d = x32.shape
    tm = min(tm, n)
    th = FFN_HIDDEN // chunks
    assert n % tm == 0 and FFN_HIDDEN % chunks == 0 and th % LANES == 0
    row = pl.BlockSpec((tm, d), lambda i, c: (i, 0))
    full = pl.BlockSpec((1, d), lambda i, c: (0, 0))
    return pl.pallas_call(
        _ffn_kernel,
        out_shape=(jax.ShapeDtypeStruct((n, d), _F32), jax.ShapeDtypeStruct((n, d), _BF16)),
        grid=(n // tm, chunks),
        in_specs=[row, row,
                  pl.BlockSpec((d, th), lambda i, c: (0, c)),
                  pl.BlockSpec((d, th), lambda i, c: (0, chunks + c)),
                  pl.BlockSpec((th, d), lambda i, c: (c, 0)),
                  full, full],
        out_specs=(row, row),
        scratch_shapes=[pltpu.VMEM((tm, d), _F32)],
        compiler_params=_params("parallel", "arbitrary"),
        name="ffn_ln3",
    )(x32, x16, w_up, w_up, w_down, g.reshape(1, d), b.reshape(1, d))


def kernel(x, mem, w_in, lam_qk, diff_subln, conv_w, conv_b, gate_w, gate_b, lru_lambda, w_br_a, w_br_b, w_br_c, w_out, ln1_g, ln1_b, xq, xkv, xo, ln2_g, ln2_b, w_up, w_down, ln3_g, ln3_b):
    bsz, seq, d = x.shape
    n = bsz * seq
    tables = _rotary_tables(seq)
    mem16 = mem.reshape(bsz * mem.shape[1], d).astype(_BF16)
    x32 = x.reshape(n, d)
    x16 = x32.astype(_BF16)
    for l in range(DEPTH):
        w = w_in[l].astype(_BF16)
        lambda_init = 0.8 - 0.6 * math.exp(-0.3 * l)
        qk_a = _project("proj_qk_a", x16, w, COL_QK_A, 1024, _BF16, seq, tables)
        v_a = _project("proj_v_a", x16, w, COL_V_A, 512, _BF16, seq)
        rxg = _project("proj_rnn", x16, w, COL_RNN, 2048, _F32, seq)
        qk_c = _project("proj_qk_c", x16, w, COL_QK_C, 1024, _BF16, seq, tables)
        v_c = _project("proj_v_c", x16, w, COL_V_C, 512, _BF16, seq)
        gl = _project("proj_gates", x16, w, COL_GATE, 3072, _F32, seq)

        o_a = _diff_attention(qk_a.reshape(bsz, seq, 1024), v_a.reshape(bsz, seq, 512),
                              lam_qk[l], diff_subln[l], lambda_init)
        o_b = _rg_lru(rxg.reshape(bsz, seq, 2048), conv_w[l], conv_b[l], gate_w[l], gate_b[l],
                      lru_lambda[l])
        o_c = _dilated_attention(qk_c.reshape(bsz, seq, 1024), v_c.reshape(bsz, seq, 512))

        x32, x16 = _merge(o_a.reshape(n, 512), o_b.reshape(n, 1024), o_c.reshape(n, 512), gl, x32,
                          w_br_a[l].astype(_BF16), w_br_b[l].astype(_BF16), w_br_c[l].astype(_BF16),
                          w_out[l].astype(_BF16), ln1_g[l], ln1_b[l])

        kv = _project("proj_mem_kv", mem16, xkv[l].astype(_BF16), 0, 2 * d, _BF16, mem.shape[1])
        x32, x16 = _cross_attention(x32.reshape(bsz, seq, d), x16.reshape(bsz, seq, d),
                                    kv.reshape(bsz, mem.shape[1], 2 * d), xq[l].astype(_BF16),
                                    xo[l].astype(_BF16), ln2_g[l], ln2_b[l])
        x32, x16 = _ffn(x32.reshape(n, d), x16.reshape(n, d), w_up[l].astype(_BF16),
                        w_down[l].astype(_BF16), ln3_g[l], ln3_b[l])
    return x32.reshape(bsz, seq, d)
```

```python
import functools
import math

import jax
import jax.numpy as jnp
from jax import lax
from jax.experimental import pallas as pl
from jax.experimental.pallas import tpu as pltpu

D_MODEL = 1024
DEPTH = 2
HEAD_DIM = 64
ROT_DIM = HEAD_DIM // 4
ROPE_THETA = 500000.0
DIFF_HEADS = 4
DIFF_WIDTH = 512
DIL_HEADS = 8
DIL_WIDTH = 512
DIL_STEPS = 128
DILATIONS = (1, 4, 16)
RNN_WIDTH = D_MODEL
LRU_BLOCK = 64
LRU_C = 8.0
CONV_WIDTH = 4
MEM_TOKENS = 256
XATTN_HEADS = 4
XATTN_HEAD_DIM = D_MODEL // XATTN_HEADS
FFN_HIDDEN = 2816
DEEPNORM_ALPHA = (2 * DEPTH) ** 0.25
LN_EPS = 1e-5

COL_QK_A = 0
COL_V_A = 1024
COL_RNN = 1536
COL_QK_C = 3584
COL_V_C = 4608
COL_GATE = 5120

LANES = 128
GATE_TILE = 256
V7X_VMEM_BYTES = 64 * 1024 * 1024
VMEM_LIMIT = V7X_VMEM_BYTES * 3 // 4

_BF16 = jnp.bfloat16
_F32 = jnp.float32
_NT = (((1,), (1,)), ((), ()))


def _params(*semantics):
    return pltpu.CompilerParams(dimension_semantics=semantics, vmem_limit_bytes=VMEM_LIMIT)


def _layer_norm(z, g, b):
    mu = jnp.mean(z, axis=-1, keepdims=True)
    d = z - mu
    var = jnp.mean(d * d, axis=-1, keepdims=True)
    return d * lax.rsqrt(var + LN_EPS) * g + b


def _proj_kernel(x_ref, w_ref, *rest, rotary):
    acc = jnp.dot(x_ref[...], w_ref[...], preferred_element_type=_F32)
    if not rotary:
        (o_ref,) = rest
        o_ref[...] = acc.astype(o_ref.dtype)
        return
    cos_ref, sdn_ref, sup_ref, o_ref = rest
    c, sdn, sup = cos_ref[...], sdn_ref[...], sup_ref[...]
    half = ROT_DIM // 2
    for j in range(acc.shape[1] // LANES):
        blk = acc[:, j * LANES:(j + 1) * LANES]
        dn = pltpu.roll(blk, LANES - half, axis=1)
        up = pltpu.roll(blk, half, axis=1)
        o_ref[:, j * LANES:(j + 1) * LANES] = (blk * c + dn * sdn + up * sup).astype(o_ref.dtype)


def _rotary_tables(seq):
    half = ROT_DIM // 2
    inv_freq = jnp.power(ROPE_THETA, -2.0 * jnp.arange(half, dtype=_F32) / ROT_DIM)
    ang = jnp.arange(seq).astype(_F32)[:, None] * inv_freq[None, :]
    cos, sin = jnp.cos(ang), jnp.sin(ang)
    ones = jnp.ones((seq, HEAD_DIM - ROT_DIM), _F32)
    zeros = jnp.zeros((seq, HEAD_DIM - ROT_DIM), _F32)
    zh = jnp.zeros((seq, half), _F32)
    c = jnp.concatenate([cos, cos, ones], axis=1)
    sdn = jnp.concatenate([-sin, zh, zeros], axis=1)
    sup = jnp.concatenate([zh, sin, zeros], axis=1)
    reps = LANES // HEAD_DIM
    return tuple(jnp.tile(t, (1, reps)) for t in (c, sdn, sup))


def _project(name, x2, w, col0, width, out_dtype, seq, tables=None, tm=1024, tn=512):
    n, k = x2.shape
    tm = min(tm, seq)
    tn = min(tn, width)
    assert n % tm == 0 and width % tn == 0 and col0 % tn == 0 and seq % tm == 0
    grid = (width // tn, n // tm)
    in_specs = [pl.BlockSpec((tm, k), lambda j, i: (i, 0)),
                pl.BlockSpec((k, tn), lambda j, i: (0, col0 // tn + j))]
    args = [x2, w]
    if tables is not None:
        tspec = pl.BlockSpec((tm, LANES), lambda j, i: (i % (seq // tm), 0))
        in_specs += [tspec] * 3
        args += list(tables)
    return pl.pallas_call(
        functools.partial(_proj_kernel, rotary=tables is not None),
        out_shape=jax.ShapeDtypeStruct((n, width), out_dtype),
        grid=grid,
        in_specs=in_specs,
        out_specs=pl.BlockSpec((tm, tn), lambda j, i: (i, j)),
        compiler_params=_params("parallel", "parallel"),
        name=name,
    )(*args)


ONES_ROWS = 16
ATTN_SUB = 256


def _column_max(x):
    rows = x.shape[0]
    while rows > 8 and rows % 2 == 0:
        rows //= 2
        x = jnp.maximum(x[:rows], x[rows:])
    return jnp.max(x, axis=0, keepdims=True)


def _diff_attn_kernel(lam_ref, g_ref, q_ref, k_ref, v_ref, o_ref, wq_scr, vt_scr, acc_scr, m_scr,
                      sa_scr, sb_scr, *, lambda_init):
    qi = pl.program_id(2)
    t = q_ref.shape[1]
    n_kv = k_ref.shape[1] // t
    vd = 2 * HEAD_DIM

    @pl.when(qi == 0)
    def _transpose_v():
        for j in range(n_kv):
            vt_scr[j, 0:vd, :] = v_ref[0, j * t:(j + 1) * t, :].astype(_F32).T.astype(_BF16)
            vt_scr[j, vd:vd + ONES_ROWS, :] = jnp.ones((ONES_ROWS, t), _BF16)

    q_t = (q_ref[0].astype(_F32) * (HEAD_DIM ** -0.5)).T
    zero = jnp.zeros((HEAD_DIM, t), _F32)
    wq_scr[:, 0:t] = jnp.concatenate([q_t[0:HEAD_DIM], zero], axis=0).astype(_BF16)
    wq_scr[:, t:2 * t] = jnp.concatenate([zero, q_t[HEAD_DIM:vd]], axis=0).astype(_BF16)
    acc_scr[...] = jnp.zeros_like(acc_scr)
    m_scr[...] = jnp.full_like(m_scr, -jnp.inf)

    chains = [slice(c * ATTN_SUB, (c + 1) * ATTN_SUB) for c in range(2 * t // ATTN_SUB)]

    def scores(j, s_ref, cols):
        k_t = k_ref[0, pl.ds(pl.multiple_of(j * t, t), t), :]
        s_ref[:, cols] = jnp.dot(k_t, wq_scr[:, cols], preferred_element_type=_F32)

    def softmax_pv(j, s_ref, cols, masked):
        s = s_ref[:, cols]
        if masked:
            key = lax.broadcasted_iota(jnp.int32, (t, ATTN_SUB), 0)
            qry = lax.broadcasted_iota(jnp.int32, (t, ATTN_SUB), 1) + cols.start % t
            s = jnp.where(key <= qry, s, -jnp.inf)
        m_prev = m_scr[:, cols]
        m_new = jnp.maximum(m_prev, _column_max(s))
        alpha = jnp.exp(m_prev - m_new)
        p = jnp.exp(s - m_new).astype(_BF16)
        acc_scr[:, cols] = alpha * acc_scr[:, cols] + jnp.dot(vt_scr[j], p, preferred_element_type=_F32)
        m_scr[:, cols] = m_new

    def stage(j_next, s_next, j, s_cur, masked):
        for cols in chains:
            if j_next is not None:
                scores(j_next, s_next, cols)
            softmax_pv(j, s_cur, cols, masked)

    for cols in chains:
        scores(0, sa_scr, cols)

    def body(jj, carry):
        j = 2 * jj
        stage(j + 1, sb_scr, j, sa_scr, False)
        stage(j + 2, sa_scr, j + 1, sb_scr, False)
        return carry

    lax.fori_loop(0, qi // 2, body, 0)

    @pl.when(qi % 2 == 0)
    def _even_tail():
        stage(None, None, qi, sa_scr, True)

    @pl.when(qi % 2 == 1)
    def _odd_tail():
        stage(qi, sb_scr, qi - 1, sa_scr, False)
        stage(None, None, qi, sb_scr, True)

    lq = lam_ref[...]
    lam = (jnp.exp(jnp.sum(lq[0:1] * lq[1:2], axis=1, keepdims=True))
           - jnp.exp(jnp.sum(lq[2:3] * lq[3:4], axis=1, keepdims=True)) + lambda_init)
    o1 = acc_scr[0:vd, 0:t] / acc_scr[vd:vd + 1, 0:t]
    o2 = acc_scr[0:vd, t:2 * t] / acc_scr[vd:vd + 1, t:2 * t]
    o = (o1 - lam * o2).T
    y = o * lax.rsqrt(jnp.mean(o * o, axis=1, keepdims=True) + LN_EPS) * g_ref[...]
    o_ref[0] = (y * (1.0 - lambda_init)).astype(o_ref.dtype)


def _diff_attention(qk, v, lam_qk, subln, lambda_init, tile=512):
    b, s, _ = qk.shape
    t = min(tile, s)
    assert s % t == 0 and (2 * t) % ATTN_SUB == 0 and t % ATTN_SUB == 0
    vd = 2 * HEAD_DIM
    k_off = DIFF_WIDTH // vd
    return pl.pallas_call(
        functools.partial(_diff_attn_kernel, lambda_init=lambda_init),
        out_shape=jax.ShapeDtypeStruct((b, s, DIFF_WIDTH), _BF16),
        grid=(b, DIFF_HEADS, s // t),
        in_specs=[
            pl.BlockSpec((4, HEAD_DIM), lambda bi, h, qi: (0, 0)),
            pl.BlockSpec((1, vd), lambda bi, h, qi: (0, 0)),
            pl.BlockSpec((1, t, vd), lambda bi, h, qi: (bi, qi, h)),
            pl.BlockSpec((1, s, vd), lambda bi, h, qi: (bi, 0, k_off + h)),
            pl.BlockSpec((1, s, vd), lambda bi, h, qi: (bi, 0, h)),
        ],
        out_specs=pl.BlockSpec((1, t, vd), lambda bi, h, qi: (bi, qi, h)),
        scratch_shapes=[pltpu.VMEM((vd, 2 * t), _BF16),
                        pltpu.VMEM((s // t, vd + ONES_ROWS, t), _BF16),
                        pltpu.VMEM((vd + ONES_ROWS, 2 * t), _F32),
                        pltpu.VMEM((1, 2 * t), _F32),
                        pltpu.VMEM((t, 2 * t), _F32),
                        pltpu.VMEM((t, 2 * t), _F32)],
        compiler_params=_params("parallel", "parallel", "arbitrary"),
        name="diff_attn",
    )(lam_qk, subln.reshape(1, vd), qk, qk, v)


HIST = 8


def _lru_kernel(rxg_ref, cw_ref, cb_ref, wbd_ref, gb_ref, lam_ref, o_ref, xbuf, a_buf, u_buf, h_ref):
    t = o_ref.shape[1]
    w = RNN_WIDTH

    @pl.when(pl.program_id(1) == 0)
    def _init():
        xbuf[0:HIST, :] = jnp.zeros((HIST, w), _F32)
        h_ref[...] = jnp.zeros_like(h_ref)

    xbuf[HIST:HIST + t, :] = rxg_ref[0, :, 0:w]
    xc = cb_ref[...]
    for tap in range(CONV_WIDTH):
        start = HIST - (CONV_WIDTH - 1) + tap
        xc = xc + xbuf[start:start + t, :] * cw_ref[tap:tap + 1, :]
    xbuf[0:HIST, :] = xbuf[t:t + HIST, :]

    xcb = xc.astype(_BF16)
    gates = []
    for g in range(2):
        parts = [jnp.dot(xcb[:, j * GATE_TILE:(j + 1) * GATE_TILE], wbd_ref[g, j],
                         preferred_element_type=_F32) for j in range(w // GATE_TILE)]
        gates.append(jnp.concatenate(parts, axis=1) + gb_ref[g:g + 1, :])
    r = jax.nn.sigmoid(gates[0])
    i = jax.nn.sigmoid(gates[1])
    neg = -lam_ref[...]
    softplus = jnp.maximum(neg, 0.0) + jnp.log1p(jnp.exp(-jnp.abs(neg)))
    log_a = -LRU_C * r * softplus
    a = jnp.exp(log_a)
    a_buf[...] = a
    u_buf[...] = jnp.sqrt(-jnp.tanh(log_a) * (1.0 + a * a)) * (i * xc)

    def body(row, h):
        h = a_buf[pl.ds(row, 1), :] * h + u_buf[pl.ds(row, 1), :]
        u_buf[pl.ds(row, 1), :] = h
        return h

    h_ref[...] = lax.fori_loop(0, t, body, h_ref[...], unroll=8)
    o_ref[0] = (u_buf[...] * jax.nn.gelu(rxg_ref[0, :, w:2 * w])).astype(o_ref.dtype)


def _block_diag_gates(gate_w):
    per = GATE_TILE // LRU_BLOCK
    n_tiles = RNN_WIDTH // GATE_TILE
    gw = gate_w.reshape(2, n_tiles, per, LRU_BLOCK, LRU_BLOCK)
    eye = jnp.eye(per, dtype=gate_w.dtype)
    dense = jnp.einsum("gtpcd,pq->gtpcqd", gw, eye)
    return dense.reshape(2, n_tiles, GATE_TILE, GATE_TILE)


def _rg_lru(rxg, conv_w, conv_b, gate_w, gate_b, lru_lambda, tile=512):
    b, s, _ = rxg.shape
    t = min(tile, s)
    assert s % t == 0
    w = RNN_WIDTH
    full = lambda shape: pl.BlockSpec(shape, lambda bi, ti: (0,) * len(shape))
    return pl.pallas_call(
        _lru_kernel,
        out_shape=jax.ShapeDtypeStruct((b, s, w), _BF16),
        grid=(b, s // t),
        in_specs=[pl.BlockSpec((1, t, 2 * w), lambda bi, ti: (bi, ti, 0)),
                  full((CONV_WIDTH, w)), full((1, w)),
                  full((2, w // GATE_TILE, GATE_TILE, GATE_TILE)), full((2, w)), full((1, w))],
        out_specs=pl.BlockSpec((1, t, w), lambda bi, ti: (bi, ti, 0)),
        scratch_shapes=[pltpu.VMEM((t + HIST, w), _F32), pltpu.VMEM((t, w), _F32),
                        pltpu.VMEM((t, w), _F32), pltpu.VMEM((1, w), _F32)],
        compiler_params=_params("parallel", "arbitrary"),
        name="rg_lru",
    )(rxg, conv_w, conv_b.reshape(1, w), _block_diag_gates(gate_w).astype(_BF16), gate_b,
      lru_lambda.reshape(1, w))


def _dil_kernel(*refs, has_prev, is_last):
    q_ref, kp_ref, kc_ref, vp_ref, vc_ref = refs[:5]
    refs = refs[5:]
    if has_prev:
        oprev_ref, lprev_ref = refs[:2]
        refs = refs[2:]
    o_ref = refs[0]
    l_ref = None if is_last else refs[1]

    n = pl.program_id(2)
    st = DIL_STEPS
    q = q_ref[0] * (HEAD_DIM ** -0.5)
    kk = jnp.concatenate([kp_ref[0], kc_ref[0]], axis=0)
    vv = jnp.concatenate([vp_ref[0], vc_ref[0]], axis=0)
    ii = lax.broadcasted_iota(jnp.int32, (st, 2 * st), 0)
    jj = lax.broadcasted_iota(jnp.int32, (st, 2 * st), 1)
    dist = st + ii - jj
    valid = (dist >= 0) & (dist <= st) & ((n > 0) | (jj >= st))
    if l_ref is not None:
        l_ref[0] = jnp.zeros(l_ref.shape[1:], _F32)
    for h in range(DIL_HEADS):
        sl = slice(h * HEAD_DIM, (h + 1) * HEAD_DIM)
        s = lax.dot_general(q[:, sl], kk[:, sl], _NT, preferred_element_type=_F32)
        s = jnp.where(valid, s, -jnp.inf)
        m = jnp.max(s, axis=1, keepdims=True)
        p = jnp.exp(s - m)
        l = jnp.sum(p, axis=1, keepdims=True)
        o_h = jnp.dot(p.astype(_BF16), vv[:, sl], preferred_element_type=_F32) / l
        lse = m + jnp.log(l)
        if has_prev:
            lp = lprev_ref[0, :, h:h + 1]
            mx = jnp.maximum(lp, lse)
            w_prev = jnp.exp(lp - mx)
            w_cur = jnp.exp(lse - mx)
            den = w_prev + w_cur
            o_h = (w_prev * oprev_ref[0, :, sl] + w_cur * o_h) / den
            lse = mx + jnp.log(den)
        o_ref[0, :, sl] = o_h.astype(o_ref.dtype)
        if l_ref is not None:
            l_ref[0, :, h:h + 1] = lse


def _dilated_pattern(qk, v, dil, prev, is_last):
    b, s, _ = qk.shape
    st = DIL_STEPS
    assert s % (dil * st) == 0
    m = s // dil
    qk_d = qk.reshape(b, m, dil * 2 * DIL_WIDTH)
    v_d = v.reshape(b, m, dil * DIL_WIDTH)
    blk = lambda width, fn: pl.BlockSpec((1, st, width), fn)
    in_specs = [
        blk(DIL_WIDTH, lambda bi, p, n: (bi, n, 2 * p)),
        blk(DIL_WIDTH, lambda bi, p, n: (bi, jnp.maximum(n - 1, 0), 2 * p + 1)),
        blk(DIL_WIDTH, lambda bi, p, n: (bi, n, 2 * p + 1)),
        blk(DIL_WIDTH, lambda bi, p, n: (bi, jnp.maximum(n - 1, 0), p)),
        blk(DIL_WIDTH, lambda bi, p, n: (bi, n, p)),
    ]
    args = [qk_d, qk_d, qk_d, v_d, v_d]
    if prev is not None:
        o_prev, l_prev = prev
        in_specs += [blk(DIL_WIDTH, lambda bi, p, n: (bi, n, p)), blk(LANES, lambda bi, p, n: (bi, n, p))]
        args += [o_prev.reshape(b, m, dil * DIL_WIDTH), l_prev.reshape(b, m, dil * LANES)]
    o_shape = jax.ShapeDtypeStruct((b, m, dil * DIL_WIDTH), _BF16 if is_last else _F32)
    o_spec = blk(DIL_WIDTH, lambda bi, p, n: (bi, n, p))
    if is_last:
        out_shape, out_specs = o_shape, o_spec
    else:
        out_shape = (o_shape, jax.ShapeDtypeStruct((b, m, dil * LANES), _F32))
        out_specs = (o_spec, blk(LANES, lambda bi, p, n: (bi, n, p)))
    out = pl.pallas_call(
        functools.partial(_dil_kernel, has_prev=prev is not None, is_last=is_last),
        out_shape=out_shape,
        grid=(b, dil, m // st),
        in_specs=in_specs,
        out_specs=out_specs,
        compiler_params=_params("parallel", "parallel", "parallel"),
        name=f"dil_attn_d{dil}",
    )(*args)
    if is_last:
        return out.reshape(b, s, DIL_WIDTH)
    return out[0].reshape(b, s, DIL_WIDTH), out[1].reshape(b, s, LANES)


def _dilated_attention(qk, v):
    prev = None
    for idx, dil in enumerate(DILATIONS):
        prev = _dilated_pattern(qk, v, dil, prev, is_last=idx == len(DILATIONS) - 1)
    return prev


def _merge_kernel(oa_ref, ob_ref, oc_ref, gl_ref, x_ref, wa_ref, wb_ref, wc_ref, wo_ref,
                  g_ref, b_ref, y32_ref, y16_ref):
    d = D_MODEL
    merged = None
    for idx, (o_ref, w_ref) in enumerate(((oa_ref, wa_ref), (ob_ref, wb_ref), (oc_ref, wc_ref))):
        branch = jnp.dot(o_ref[...], w_ref[...], preferred_element_type=_F32)
        term = jax.nn.sigmoid(gl_ref[:, idx * d:(idx + 1) * d]) * branch
        merged = term if merged is None else merged + term
    h = jnp.dot(merged.astype(_BF16), wo_ref[...], preferred_element_type=_F32)
    y = _layer_norm(DEEPNORM_ALPHA * x_ref[...] + h, g_ref[...], b_ref[...])
    y32_ref[...] = y
    y16_ref[...] = y.astype(_BF16)


def _merge(o_a, o_b, o_c, gl, x, w_a, w_b, w_c, w_o, g, b, tm=512):
    n, d = x.shape
    tm = min(tm, n)
    assert n % tm == 0
    row = lambda width: pl.BlockSpec((tm, width), lambda i: (i, 0))
    full = lambda shape: pl.BlockSpec(shape, lambda i: (0, 0))
    return pl.pallas_call(
        _merge_kernel,
        out_shape=(jax.ShapeDtypeStruct((n, d), _F32), jax.ShapeDtypeStruct((n, d), _BF16)),
        grid=(n // tm,),
        in_specs=[row(DIFF_WIDTH), row(RNN_WIDTH), row(DIL_WIDTH), row(3 * d), row(d),
                  full((DIFF_WIDTH, d)), full((RNN_WIDTH, d)), full((DIL_WIDTH, d)), full((d, d)),
                  full((1, d)), full((1, d))],
        out_specs=(row(d), row(d)),
        compiler_params=_params("parallel"),
        name="merge_ln1",
    )(o_a, o_b, o_c, gl, x, w_a, w_b, w_c, w_o, g.reshape(1, d), b.reshape(1, d))


def _xattn_kernel(x32_ref, x16_ref, kv_ref, wq_ref, wo_ref, g_ref, b_ref, y32_ref, y16_ref):
    d = D_MODEL
    hd = XATTN_HEAD_DIM
    q = jnp.dot(x16_ref[0], wq_ref[...], preferred_element_type=_F32)
    q = (q * (hd ** -0.5)).astype(_BF16)
    heads = []
    for h in range(XATTN_HEADS):
        k_h = kv_ref[0, :, h * hd:(h + 1) * hd]
        v_h = kv_ref[0, :, d + h * hd:d + (h + 1) * hd]
        s = lax.dot_general(q[:, h * hd:(h + 1) * hd], k_h, _NT, preferred_element_type=_F32)
        p = jnp.exp(s - jnp.max(s, axis=1, keepdims=True))
        l = jnp.sum(p, axis=1, keepdims=True)
        heads.append(jnp.dot(p.astype(_BF16), v_h, preferred_element_type=_F32) / l)
    o = jnp.concatenate(heads, axis=1).astype(_BF16)
    h_out = jnp.dot(o, wo_ref[...], preferred_element_type=_F32)
    y = _layer_norm(DEEPNORM_ALPHA * x32_ref[0] + h_out, g_ref[...], b_ref[...])
    y32_ref[0] = y
    y16_ref[0] = y.astype(_BF16)


def _cross_attention(x32, x16, kv, wq, wo, g, b, tm=512):
    bsz, s, d = x32.shape
    tm = min(tm, s)
    assert s % tm == 0
    row = pl.BlockSpec((1, tm, d), lambda bi, i: (bi, i, 0))
    full = lambda shape: pl.BlockSpec(shape, lambda bi, i: (0, 0))
    return pl.pallas_call(
        _xattn_kernel,
        out_shape=(jax.ShapeDtypeStruct((bsz, s, d), _F32), jax.ShapeDtypeStruct((bsz, s, d), _BF16)),
        grid=(bsz, s // tm),
        in_specs=[row, row, pl.BlockSpec((1, MEM_TOKENS, 2 * d), lambda bi, i: (bi, 0, 0)),
                  full((d, d)), full((d, d)), full((1, d)), full((1, d))],
        out_specs=(row, row),
        compiler_params=_params("parallel", "parallel"),
        name="xattn_ln2",
    )(x32, x16, kv, wq, wo, g.reshape(1, d), b.reshape(1, d))


def _ffn_kernel(x32_ref, x16_ref, wg_ref, wu_ref, wd_ref, g_ref, b_ref, y32_ref, y16_ref, acc_ref):
    c = pl.program_id(1)

    @pl.when(c == 0)
    def _init():
        acc_ref[...] = jnp.zeros_like(acc_ref)

    x = x16_ref[...]
    gate = jnp.dot(x, wg_ref[...], preferred_element_type=_F32)
    up = jnp.dot(x, wu_ref[...], preferred_element_type=_F32)
    hidden = (jax.nn.silu(gate) * up).astype(_BF16)
    acc_ref[...] += jnp.dot(hidden, wd_ref[...], preferred_element_type=_F32)

    @pl.when(c == pl.num_programs(1) - 1)
    def _fin():
        y = _layer_norm(DEEPNORM_ALPHA * x32_ref[...] + acc_ref[...], g_ref[...], b_ref[...])
        y32_ref[...] = y
        y16_ref[...] = y.astype(_BF16)


def _ffn(x32, x16, w_up, w_down, g, b, tm=512, chunks=2):
    n, d = x32.shape
    tm = min(tm, n)
    th = FFN_HIDDEN // chunks
    assert n % tm == 0 and FFN_HIDDEN % chunks == 0 and th % LANES == 0
    row = pl.BlockSpec((tm, d), lambda i, c: (i, 0))
    full = pl.BlockSpec((1, d), lambda i, c: (0, 0))
    return pl.pallas_call(
        _ffn_kernel,
        out_shape=(jax.ShapeDtypeStruct((n, d), _F32), jax.ShapeDtypeStruct((n, d), _BF16)),
        grid=(n // tm, chunks),
        in_specs=[row, row,
                  pl.BlockSpec((d, th), lambda i, c: (0, c)),
                  pl.BlockSpec((d, th), lambda i, c: (0, chunks + c)),
                  pl.BlockSpec((th, d), lambda i, c: (c, 0)),
                  full, full],
        out_specs=(row, row),
        scratch_shapes=[pltpu.VMEM((tm, d), _F32)],
        compiler_params=_params("parallel", "arbitrary"),
        name="ffn_ln3",
    )(x32, x16, w_up, w_up, w_down, g.reshape(1, d), b.reshape(1, d))


def kernel(x, mem, w_in, lam_qk, diff_subln, conv_w, conv_b, gate_w, gate_b, lru_lambda, w_br_a, w_br_b, w_br_c, w_out, ln1_g, ln1_b, xq, xkv, xo, ln2_g, ln2_b, w_up, w_down, ln3_g, ln3_b):
    bsz, seq, d = x.shape
    n = bsz * seq
    tables = _rotary_tables(seq)
    mem16 = mem.reshape(bsz * mem.shape[1], d).astype(_BF16)
    x32 = x.reshape(n, d)
    x16 = x32.astype(_BF16)
    for l in range(DEPTH):
        w = w_in[l].astype(_BF16)
        lambda_init = 0.8 - 0.6 * math.exp(-0.3 * l)
        qk_a = _project("proj_qk_a", x16, w, COL_QK_A, 1024, _BF16, seq, tables)
        v_a = _project("proj_v_a", x16, w, COL_V_A, 512, _BF16, seq)
        rxg = _project("proj_rnn", x16, w, COL_RNN, 2048, _F32, seq)
        qk_c = _project("proj_qk_c", x16, w, COL_QK_C, 1024, _BF16, seq, tables)
        v_c = _project("proj_v_c", x16, w, COL_V_C, 512, _BF16, seq)
        gl = _project("proj_gates", x16, w, COL_GATE, 3072, _F32, seq)

        o_a = _diff_attention(qk_a.reshape(bsz, seq, 1024), v_a.reshape(bsz, seq, 512),
                              lam_qk[l], diff_subln[l], lambda_init)
        o_b = _rg_lru(rxg.reshape(bsz, seq, 2048), conv_w[l], conv_b[l], gate_w[l], gate_b[l],
                      lru_lambda[l])
        o_c = _dilated_attention(qk_c.reshape(bsz, seq, 1024), v_c.reshape(bsz, seq, 512))

        x32, x16 = _merge(o_a.reshape(n, 512), o_b.reshape(n, 1024), o_c.reshape(n, 512), gl, x32,
                          w_br_a[l].astype(_BF16), w_br_b[l].astype(_BF16), w_br_c[l].astype(_BF16),
                          w_out[l].astype(_BF16), ln1_g[l], ln1_b[l])

        kv = _project("proj_mem_kv", mem16, xkv[l].astype(_BF16), 0, 2 * d, _BF16, mem.shape[1])
        x32, x16 = _cross_attention(x32.reshape(bsz, seq, d), x16.reshape(bsz, seq, d),
                                    kv.reshape(bsz, mem.shape[1], 2 * d), xq[l].astype(_BF16),
                                    xo[l].astype(_BF16), ln2_g[l], ln2_b[l])
        x32, x16 = _ffn(x32.reshape(n, d), x16.reshape(n, d), w_up[l].astype(_BF16),
                        w_down[l].astype(_BF16), ln3_g[l], ln3_b[l])
    return x32.reshape(bsz, seq, d)
```

```python
import functools
import math

import jax
import jax.numpy as jnp
from jax import lax
from jax.experimental import pallas as pl
from jax.experimental.pallas import tpu as pltpu

D_MODEL = 1024
DEPTH = 2
HEAD_DIM = 64
ROT_DIM = HEAD_DIM // 4
ROPE_THETA = 500000.0
DIFF_HEADS = 4
DIFF_WIDTH = 512
DIL_HEADS = 8
DIL_WIDTH = 512
DIL_STEPS = 128
DILATIONS = (1, 4, 16)
RNN_WIDTH = D_MODEL
LRU_BLOCK = 64
LRU_C = 8.0
CONV_WIDTH = 4
MEM_TOKENS = 256
XATTN_HEADS = 4
XATTN_HEAD_DIM = D_MODEL // XATTN_HEADS
FFN_HIDDEN = 2816
DEEPNORM_ALPHA = (2 * DEPTH) ** 0.25
LN_EPS = 1e-5

COL_QK_A = 0
COL_RNN = 1536
COL_QK_C = 3584
COL_GATE = 5120

LANES = 128
GATE_TILE = 256
V7X_VMEM_BYTES = 64 * 1024 * 1024
VMEM_LIMIT = V7X_VMEM_BYTES * 3 // 4

_BF16 = jnp.bfloat16
_F32 = jnp.float32
_NT = (((1,), (1,)), ((), ()))


def _params(*semantics):
    return pltpu.CompilerParams(dimension_semantics=semantics, vmem_limit_bytes=VMEM_LIMIT)


def _layer_norm(z, g, b):
    mu = jnp.mean(z, axis=-1, keepdims=True)
    d = z - mu
    var = jnp.mean(d * d, axis=-1, keepdims=True)
    return d * lax.rsqrt(var + LN_EPS) * g + b


def _proj_kernel(x_ref, w_ref, *rest, rotary_blocks):
    acc = jnp.dot(x_ref[...], w_ref[...], preferred_element_type=_F32)
    if not rotary_blocks:
        (o_ref,) = rest
        o_ref[...] = acc.astype(o_ref.dtype)
        return
    cos_ref, sdn_ref, sup_ref, o_ref = rest
    col_block = pl.program_id(0)

    @pl.when(col_block >= rotary_blocks)
    def _plain():
        o_ref[...] = acc.astype(o_ref.dtype)

    @pl.when(col_block < rotary_blocks)
    def _rotary():
        c, sdn, sup = cos_ref[...], sdn_ref[...], sup_ref[...]
        half = ROT_DIM // 2
        for j in range(acc.shape[1] // LANES):
            blk = acc[:, j * LANES:(j + 1) * LANES]
            dn = pltpu.roll(blk, LANES - half, axis=1)
            up = pltpu.roll(blk, half, axis=1)
            o_ref[:, j * LANES:(j + 1) * LANES] = (blk * c + dn * sdn + up * sup).astype(o_ref.dtype)


def _rotary_tables(seq):
    half = ROT_DIM // 2
    inv_freq = jnp.power(ROPE_THETA, -2.0 * jnp.arange(half, dtype=_F32) / ROT_DIM)
    ang = jnp.arange(seq).astype(_F32)[:, None] * inv_freq[None, :]
    cos, sin = jnp.cos(ang), jnp.sin(ang)
    ones = jnp.ones((seq, HEAD_DIM - ROT_DIM), _F32)
    zeros = jnp.zeros((seq, HEAD_DIM - ROT_DIM), _F32)
    zh = jnp.zeros((seq, half), _F32)
    c = jnp.concatenate([cos, cos, ones], axis=1)
    sdn = jnp.concatenate([-sin, zh, zeros], axis=1)
    sup = jnp.concatenate([zh, sin, zeros], axis=1)
    reps = LANES // HEAD_DIM
    return tuple(jnp.tile(t, (1, reps)) for t in (c, sdn, sup))


def _project(name, x2, w, col0, width, out_dtype, seq, tables=None, rotary_width=0, tm=1024, tn=512):
    n, k = x2.shape
    tm = min(tm, seq)
    tn = min(tn, width)
    assert n % tm == 0 and width % tn == 0 and col0 % tn == 0 and seq % tm == 0
    assert rotary_width % tn == 0 and (tables is None) == (rotary_width == 0)
    grid = (width // tn, n // tm)
    in_specs = [pl.BlockSpec((tm, k), lambda j, i: (i, 0)),
                pl.BlockSpec((k, tn), lambda j, i: (0, col0 // tn + j))]
    args = [x2, w]
    if tables is not None:
        tspec = pl.BlockSpec((tm, LANES), lambda j, i: (i % (seq // tm), 0))
        in_specs += [tspec] * 3
        args += list(tables)
    return pl.pallas_call(
        functools.partial(_proj_kernel, rotary_blocks=rotary_width // tn),
        out_shape=jax.ShapeDtypeStruct((n, width), out_dtype),
        grid=grid,
        in_specs=in_specs,
        out_specs=pl.BlockSpec((tm, tn), lambda j, i: (i, j)),
        compiler_params=_params("parallel", "parallel"),
        name=name,
    )(*args)


ONES_ROWS = 16
ATTN_SUB = 256


def _column_max(x):
    rows = x.shape[0]
    while rows > 8 and rows % 2 == 0:
        rows //= 2
        x = jnp.maximum(x[:rows], x[rows:])
    return jnp.max(x, axis=0, keepdims=True)


def _diff_attn_kernel(lam_ref, g_ref, q_ref, k_ref, v_ref, o_ref, wq_scr, vt_scr, acc_scr, m_scr,
                      sa_scr, sb_scr, *, lambda_init):
    qi = pl.program_id(2)
    t = q_ref.shape[1]
    n_kv = k_ref.shape[1] // t
    vd = 2 * HEAD_DIM

    @pl.when(qi == 0)
    def _transpose_v():
        for j in range(n_kv):
            vt_scr[j, 0:vd, :] = v_ref[0, j * t:(j + 1) * t, :].astype(_F32).T.astype(_BF16)
            vt_scr[j, vd:vd + ONES_ROWS, :] = jnp.ones((ONES_ROWS, t), _BF16)

    q_t = (q_ref[0].astype(_F32) * (HEAD_DIM ** -0.5)).T
    zero = jnp.zeros((HEAD_DIM, t), _F32)
    wq_scr[:, 0:t] = jnp.concatenate([q_t[0:HEAD_DIM], zero], axis=0).astype(_BF16)
    wq_scr[:, t:2 * t] = jnp.concatenate([zero, q_t[HEAD_DIM:vd]], axis=0).astype(_BF16)
    acc_scr[...] = jnp.zeros_like(acc_scr)
    m_scr[...] = jnp.full_like(m_scr, -jnp.inf)

    chains = [slice(c * ATTN_SUB, (c + 1) * ATTN_SUB) for c in range(2 * t // ATTN_SUB)]

    def scores(j, s_ref, cols):
        k_t = k_ref[0, pl.ds(pl.multiple_of(j * t, t), t), :]
        s_ref[:, cols] = jnp.dot(k_t, wq_scr[:, cols], preferred_element_type=_F32)

    def softmax_pv(j, s_ref, cols, masked):
        s = s_ref[:, cols]
        if masked:
            key = lax.broadcasted_iota(jnp.int32, (t, ATTN_SUB), 0)
            qry = lax.broadcasted_iota(jnp.int32, (t, ATTN_SUB), 1) + cols.start % t
            s = jnp.where(key <= qry, s, -jnp.inf)
        m_prev = m_scr[:, cols]
        m_new = jnp.maximum(m_prev, _column_max(s))
        alpha = jnp.exp(m_prev - m_new)
        p = jnp.exp(s - m_new).astype(_BF16)
        acc_scr[:, cols] = alpha * acc_scr[:, cols] + jnp.dot(vt_scr[j], p, preferred_element_type=_F32)
        m_scr[:, cols] = m_new

    def stage(j_next, s_next, j, s_cur, masked):
        for cols in chains:
            if j_next is not None:
                scores(j_next, s_next, cols)
            softmax_pv(j, s_cur, cols, masked)

    for cols in chains:
        scores(0, sa_scr, cols)

    def body(jj, carry):
        j = 2 * jj
        stage(j + 1, sb_scr, j, sa_scr, False)
        stage(j + 2, sa_scr, j + 1, sb_scr, False)
        return carry

    lax.fori_loop(0, qi // 2, body, 0)

    @pl.when(qi % 2 == 0)
    def _even_tail():
        stage(None, None, qi, sa_scr, True)

    @pl.when(qi % 2 == 1)
    def _odd_tail():
        stage(qi, sb_scr, qi - 1, sa_scr, False)
        stage(None, None, qi, sb_scr, True)

    lq = lam_ref[...]
    lam = (jnp.exp(jnp.sum(lq[0:1] * lq[1:2], axis=1, keepdims=True))
           - jnp.exp(jnp.sum(lq[2:3] * lq[3:4], axis=1, keepdims=True)) + lambda_init)
    o1 = acc_scr[0:vd, 0:t] / acc_scr[vd:vd + 1, 0:t]
    o2 = acc_scr[0:vd, t:2 * t] / acc_scr[vd:vd + 1, t:2 * t]
    o = (o1 - lam * o2).T
    y = o * lax.rsqrt(jnp.mean(o * o, axis=1, keepdims=True) + LN_EPS) * g_ref[...]
    o_ref[0] = (y * (1.0 - lambda_init)).astype(o_ref.dtype)


def _diff_attention(qkv, lam_qk, subln, lambda_init, tile=512):
    b, s, _ = qkv.shape
    t = min(tile, s)
    assert s % t == 0 and (2 * t) % ATTN_SUB == 0 and t % ATTN_SUB == 0
    vd = 2 * HEAD_DIM
    k_off = DIFF_WIDTH // vd
    return pl.pallas_call(
        functools.partial(_diff_attn_kernel, lambda_init=lambda_init),
        out_shape=jax.ShapeDtypeStruct((b, s, DIFF_WIDTH), _BF16),
        grid=(b, DIFF_HEADS, s // t),
        in_specs=[
            pl.BlockSpec((4, HEAD_DIM), lambda bi, h, qi: (0, 0)),
            pl.BlockSpec((1, vd), lambda bi, h, qi: (0, 0)),
            pl.BlockSpec((1, t, vd), lambda bi, h, qi: (bi, qi, h)),
            pl.BlockSpec((1, s, vd), lambda bi, h, qi: (bi, 0, k_off + h)),
            pl.BlockSpec((1, s, vd), lambda bi, h, qi: (bi, 0, 2 * k_off + h)),
        ],
        out_specs=pl.BlockSpec((1, t, vd), lambda bi, h, qi: (bi, qi, h)),
        scratch_shapes=[pltpu.VMEM((vd, 2 * t), _BF16),
                        pltpu.VMEM((s // t, vd + ONES_ROWS, t), _BF16),
                        pltpu.VMEM((vd + ONES_ROWS, 2 * t), _F32),
                        pltpu.VMEM((1, 2 * t), _F32),
                        pltpu.VMEM((t, 2 * t), _F32),
                        pltpu.VMEM((t, 2 * t), _F32)],
        compiler_params=_params("parallel", "parallel", "arbitrary"),
        name="diff_attn",
    )(lam_qk, subln.reshape(1, vd), qkv, qkv, qkv)


HIST = 8


def _lru_kernel(rxg_ref, cw_ref, cb_ref, wbd_ref, gb_ref, lam_ref, o_ref, xbuf, a_buf, u_buf, h_ref):
    t = o_ref.shape[1]
    w = RNN_WIDTH

    @pl.when(pl.program_id(1) == 0)
    def _init():
        xbuf[0:HIST, :] = jnp.zeros((HIST, w), _F32)
        h_ref[...] = jnp.zeros_like(h_ref)

    xbuf[HIST:HIST + t, :] = rxg_ref[0, :, 0:w]
    xc = cb_ref[...]
    for tap in range(CONV_WIDTH):
        start = HIST - (CONV_WIDTH - 1) + tap
        xc = xc + xbuf[start:start + t, :] * cw_ref[tap:tap + 1, :]
    xbuf[0:HIST, :] = xbuf[t:t + HIST, :]

    xcb = xc.astype(_BF16)
    gates = []
    for g in range(2):
        parts = [jnp.dot(xcb[:, j * GATE_TILE:(j + 1) * GATE_TILE], wbd_ref[g, j],
                         preferred_element_type=_F32) for j in range(w // GATE_TILE)]
        gates.append(jnp.concatenate(parts, axis=1) + gb_ref[g:g + 1, :])
    r = jax.nn.sigmoid(gates[0])
    i = jax.nn.sigmoid(gates[1])
    neg = -lam_ref[...]
    softplus = jnp.maximum(neg, 0.0) + jnp.log1p(jnp.exp(-jnp.abs(neg)))
    log_a = -LRU_C * r * softplus
    a = jnp.exp(log_a)
    a_buf[...] = a
    u_buf[...] = jnp.sqrt(-jnp.tanh(log_a) * (1.0 + a * a)) * (i * xc)

    def body(row, h):
        h = a_buf[pl.ds(row, 1), :] * h + u_buf[pl.ds(row, 1), :]
        u_buf[pl.ds(row, 1), :] = h
        return h

    h_ref[...] = lax.fori_loop(0, t, body, h_ref[...], unroll=8)
    o_ref[0] = (u_buf[...] * jax.nn.gelu(rxg_ref[0, :, w:2 * w])).astype(o_ref.dtype)


def _block_diag_gates(gate_w):
    per = GATE_TILE // LRU_BLOCK
    n_tiles = RNN_WIDTH // GATE_TILE
    gw = gate_w.reshape(2, n_tiles, per, LRU_BLOCK, LRU_BLOCK)
    eye = jnp.eye(per, dtype=gate_w.dtype)
    dense = jnp.einsum("gtpcd,pq->gtpcqd", gw, eye)
    return dense.reshape(2, n_tiles, GATE_TILE, GATE_TILE)


def _rg_lru(rxg, conv_w, conv_b, gate_w, gate_b, lru_lambda, tile=512):
    b, s, _ = rxg.shape
    t = min(tile, s)
    assert s % t == 0
    w = RNN_WIDTH
    full = lambda shape: pl.BlockSpec(shape, lambda bi, ti: (0,) * len(shape))
    return pl.pallas_call(
        _lru_kernel,
        out_shape=jax.ShapeDtypeStruct((b, s, w), _BF16),
        grid=(b, s // t),
        in_specs=[pl.BlockSpec((1, t, 2 * w), lambda bi, ti: (bi, ti, 0)),
                  full((CONV_WIDTH, w)), full((1, w)),
                  full((2, w // GATE_TILE, GATE_TILE, GATE_TILE)), full((2, w)), full((1, w))],
        out_specs=pl.BlockSpec((1, t, w), lambda bi, ti: (bi, ti, 0)),
        scratch_shapes=[pltpu.VMEM((t + HIST, w), _F32), pltpu.VMEM((t, w), _F32),
                        pltpu.VMEM((t, w), _F32), pltpu.VMEM((1, w), _F32)],
        compiler_params=_params("parallel", "arbitrary"),
        name="rg_lru",
    )(rxg, conv_w, conv_b.reshape(1, w), _block_diag_gates(gate_w).astype(_BF16), gate_b,
      lru_lambda.reshape(1, w))


DIL_TILE = DILATIONS[-1] * DIL_STEPS


def _rows(start, size, stride):
    return pl.ds(start, size) if stride == 1 else pl.ds(start, size, stride=stride)


def _dil_kernel(q_ref, k_ref, v_ref, o_ref, qf, kf, vf, o_run, l_run):
    i = pl.program_id(1)
    t = DIL_TILE
    st = DIL_STEPS
    hd = HEAD_DIM

    pairs = DIL_HEADS // 2

    @pl.when(i == 0)
    def _no_history():
        kf[:, 0:t, :] = jnp.zeros((pairs, t, LANES), _F32)
        vf[:, 0:t, :] = jnp.zeros((pairs, t, LANES), _F32)

    @pl.when(i > 0)
    def _shift_history():
        kf[:, 0:t, :] = kf[:, t:2 * t, :]
        vf[:, 0:t, :] = vf[:, t:2 * t, :]

    for pair in range(pairs):
        lanes = slice(pair * LANES, (pair + 1) * LANES)
        qf[pair] = q_ref[0, :, lanes].astype(_F32) * (hd ** -0.5)
        kf[pair, t:2 * t, :] = k_ref[0, :, lanes].astype(_F32)
        vf[pair, t:2 * t, :] = v_ref[0, :, lanes].astype(_F32)

    key = lax.broadcasted_iota(jnp.int32, (2 * st, 2 * st), 0)
    qry = lax.broadcasted_iota(jnp.int32, (2 * st, 2 * st), 1) % st
    dist = st + qry - key
    band = (dist >= 0) & (dist <= st)
    own_block = key >= st
    zero_q = jnp.zeros((hd, st), _F32)
    ones = jnp.ones((ONES_ROWS, 2 * st), _F32)

    def block(idx, dil, merge):
        phase = idx % dil
        n = idx // dil
        start = phase + dil * st * n
        q_rows = _rows(start, st, dil)
        kv_rows = _rows(t + start - dil * st, 2 * st, dil)
        valid = band & (own_block | (i > 0) | (n > 0))
        scores = []
        for pair in range(pairs):
            q_t = qf[pair, q_rows, :].T
            k_b = kf[pair, kv_rows, :].astype(_BF16)
            w = jnp.concatenate(
                [jnp.concatenate([q_t[0:hd], zero_q], axis=1),
                 jnp.concatenate([zero_q, q_t[hd:2 * hd]], axis=1)], axis=0).astype(_BF16)
            scores.append(jnp.dot(k_b, w, preferred_element_type=_F32))
        o_parts, lse_rows = [], []
        for pair in range(pairs):
            v_t = vf[pair, kv_rows, :].T
            s = jnp.where(valid, scores[pair], -jnp.inf)
            m = _column_max(s)
            p = jnp.exp(s - m).astype(_BF16)
            lhs = jnp.concatenate([v_t[0:hd], ones, v_t[hd:2 * hd], ones], axis=0).astype(_BF16)
            pv = jnp.dot(lhs, p, preferred_element_type=_F32)
            for hh in range(2):
                rr = (hd + ONES_ROWS) * hh
                cc = slice(st * hh, st * (hh + 1))
                l = pv[rr + hd:rr + hd + 1, cc]
                o_parts.append(pv[rr:rr + hd, cc] / l)
                lse_rows.append(m[:, cc] + jnp.log(l))
        lse = jnp.concatenate(lse_rows, axis=0)
        if merge:
            run_l = l_run[q_rows, :].T[0:DIL_HEADS]
            mx = jnp.maximum(run_l, lse)
            w_run = jnp.exp(run_l - mx)
            w_cur = jnp.exp(lse - mx)
            den = w_run + w_cur
            a = w_run / den
            b = w_cur / den
            lse = mx + jnp.log(den)
        for pair in range(pairs):
            halves = o_parts[2 * pair:2 * pair + 2]
            if merge:
                run_o_t = o_run[pair, q_rows, :].T
                halves = [a[h:h + 1] * run_o_t[hh * hd:(hh + 1) * hd] + b[h:h + 1] * halves[hh]
                          for hh, h in enumerate((2 * pair, 2 * pair + 1))]
            o_run[pair, q_rows, :] = jnp.concatenate(halves, axis=0).T
        pad = jnp.zeros((LANES - DIL_HEADS, st), _F32)
        l_run[q_rows, :] = jnp.concatenate([lse, pad], axis=0).T

    for idx_p, dil in enumerate(DILATIONS):
        def body(idx, carry, dil=dil, merge=idx_p > 0):
            block(idx, dil, merge)
            return carry
        lax.fori_loop(0, t // st, body, 0)

    for pair in range(pairs):
        o_ref[0, :, pair * LANES:(pair + 1) * LANES] = o_run[pair].astype(o_ref.dtype)


def _dilated_attention(qkv):
    b, s, _ = qkv.shape
    t = DIL_TILE
    assert s % t == 0
    return pl.pallas_call(
        _dil_kernel,
        out_shape=jax.ShapeDtypeStruct((b, s, DIL_WIDTH), _BF16),
        grid=(b, s // t),
        in_specs=[pl.BlockSpec((1, t, DIL_WIDTH), lambda bi, i: (bi, i, 0)),
                  pl.BlockSpec((1, t, DIL_WIDTH), lambda bi, i: (bi, i, 1)),
                  pl.BlockSpec((1, t, DIL_WIDTH), lambda bi, i: (bi, i, 2))],
        out_specs=pl.BlockSpec((1, t, DIL_WIDTH), lambda bi, i: (bi, i, 0)),
        scratch_shapes=[pltpu.VMEM((DIL_WIDTH // LANES, t, LANES), _F32),
                        pltpu.VMEM((DIL_WIDTH // LANES, 2 * t, LANES), _F32),
                        pltpu.VMEM((DIL_WIDTH // LANES, 2 * t, LANES), _F32),
                        pltpu.VMEM((DIL_WIDTH // LANES, t, LANES), _F32),
                        pltpu.VMEM((t, LANES), _F32)],
        compiler_params=_params("parallel", "arbitrary"),
        name="dil_attn",
    )(qkv, qkv, qkv)


def _merge_kernel(oa_ref, ob_ref, oc_ref, x16_ref, x_ref, wg_ref, wa_ref, wb_ref, wc_ref, wo_ref,
                  g_ref, b_ref, y32_ref, y16_ref):
    merged = None
    x16 = x16_ref[...]
    for idx, (o_ref, w_ref) in enumerate(((oa_ref, wa_ref), (ob_ref, wb_ref), (oc_ref, wc_ref))):
        gate = jnp.dot(x16, wg_ref[idx], preferred_element_type=_F32)
        branch = jnp.dot(o_ref[...], w_ref[...], preferred_element_type=_F32)
        term = jax.nn.sigmoid(gate) * branch
        merged = term if merged is None else merged + term
    h = jnp.dot(merged.astype(_BF16), wo_ref[...], preferred_element_type=_F32)
    y = _layer_norm(DEEPNORM_ALPHA * x_ref[...] + h, g_ref[...], b_ref[...])
    y32_ref[...] = y
    y16_ref[...] = y.astype(_BF16)


def _merge(o_a, o_b, o_c, x16, x, w_gates, w_a, w_b, w_c, w_o, g, b, tm=256):
    n, d = x.shape
    tm = min(tm, n)
    assert n % tm == 0
    row = lambda width: pl.BlockSpec((tm, width), lambda i: (i, 0))
    full = lambda shape: pl.BlockSpec(shape, lambda i: (0,) * len(shape))
    return pl.pallas_call(
        _merge_kernel,
        out_shape=(jax.ShapeDtypeStruct((n, d), _F32), jax.ShapeDtypeStruct((n, d), _BF16)),
        grid=(n // tm,),
        in_specs=[row(DIFF_WIDTH), row(RNN_WIDTH), row(DIL_WIDTH), row(d), row(d),
                  full((3, d, d)), full((DIFF_WIDTH, d)), full((RNN_WIDTH, d)), full((DIL_WIDTH, d)),
                  full((d, d)), full((1, d)), full((1, d))],
        out_specs=(row(d), row(d)),
        compiler_params=_params("parallel"),
        name="merge_ln1",
    )(o_a, o_b, o_c, x16, x, w_gates, w_a, w_b, w_c, w_o, g.reshape(1, d), b.reshape(1, d))


def _xattn_kernel(x32_ref, x16_ref, kv_ref, wq_ref, wo_ref, g_ref, b_ref, y32_ref, y16_ref):
    d = D_MODEL
    hd = XATTN_HEAD_DIM
    q = jnp.dot(x16_ref[0], wq_ref[...], preferred_element_type=_F32)
    q = (q * (hd ** -0.5)).astype(_BF16)
    heads = []
    for h in range(XATTN_HEADS):
        k_h = kv_ref[0, :, h * hd:(h + 1) * hd]
        v_h = kv_ref[0, :, d + h * hd:d + (h + 1) * hd]
        s = lax.dot_general(q[:, h * hd:(h + 1) * hd], k_h, _NT, preferred_element_type=_F32)
        p = jnp.exp(s - jnp.max(s, axis=1, keepdims=True))
        l = jnp.sum(p, axis=1, keepdims=True)
        heads.append(jnp.dot(p.astype(_BF16), v_h, preferred_element_type=_F32) / l)
    o = jnp.concatenate(heads, axis=1).astype(_BF16)
    h_out = jnp.dot(o, wo_ref[...], preferred_element_type=_F32)
    y = _layer_norm(DEEPNORM_ALPHA * x32_ref[0] + h_out, g_ref[...], b_ref[...])
    y32_ref[0] = y
    y16_ref[0] = y.astype(_BF16)


def _cross_attention(x32, x16, kv, wq, wo, g, b, tm=512):
    bsz, s, d = x32.shape
    tm = min(tm, s)
    assert s % tm == 0
    row = pl.BlockSpec((1, tm, d), lambda bi, i: (bi, i, 0))
    full = lambda shape: pl.BlockSpec(shape, lambda bi, i: (0, 0))
    return pl.pallas_call(
        _xattn_kernel,
        out_shape=(jax.ShapeDtypeStruct((bsz, s, d), _F32), jax.ShapeDtypeStruct((bsz, s, d), _BF16)),
        grid=(bsz, s // tm),
        in_specs=[row, row, pl.BlockSpec((1, MEM_TOKENS, 2 * d), lambda bi, i: (bi, 0, 0)),
                  full((d, d)), full((d, d)), full((1, d)), full((1, d))],
        out_specs=(row, row),
        compiler_params=_params("parallel", "parallel"),
        name="xattn_ln2",
    )(x32, x16, kv, wq, wo, g.reshape(1, d), b.reshape(1, d))


def _ffn_kernel(x32_ref, x16_ref, wg_ref, wu_ref, wd_ref, g_ref, b_ref, y32_ref, y16_ref, acc_ref):
    c = pl.program_id(1)

    @pl.when(c == 0)
    def _init():
        acc_ref[...] = jnp.zeros_like(acc_ref)

    x = x16_ref[...]
    gate = jnp.dot(x, wg_ref[...], preferred_element_type=_F32)
    up = jnp.dot(x, wu_ref[...], preferred_element_type=_F32)
    hidden = (jax.nn.silu(gate) * up).astype(_BF16)
    acc_ref[...] += jnp.dot(hidden, wd_ref[...], preferred_element_type=_F32)

    @pl.when(c == pl.num_programs(1) - 1)
    def _fin():
        y = _layer_norm(DEEPNORM_ALPHA * x32_ref[...] + acc_ref[...], g_ref[...], b_ref[...])
        y32_ref[...] = y
        y16_ref[...] = y.astype(_BF16)


def _ffn(x32, x16, w_up, w_down, g, b, tm=512, chunks=2):
    n, d = x32.shape
    tm = min(tm, n)
    th = FFN_HIDDEN // chunks
    assert n % tm == 0 and FFN_HIDDEN % chunks == 0 and th % LANES == 0
    row = pl.BlockSpec((tm, d), lambda i, c: (i, 0))
    full = pl.BlockSpec((1, d), lambda i, c: (0, 0))
    return pl.pallas_call(
        _ffn_kernel,
        out_shape=(jax.ShapeDtypeStruct((n, d), _F32), jax.ShapeDtypeStruct((n, d), _BF16)),
        grid=(n // tm, chunks),
        in_specs=[row, row,
                  pl.BlockSpec((d, th), lambda i, c: (0, c)),
                  pl.BlockSpec((d, th), lambda i, c: (0, chunks + c)),
                  pl.BlockSpec((th, d), lambda i, c: (c, 0)),
                  full, full],
        out_specs=(row, row),
        scratch_shapes=[pltpu.VMEM((tm, d), _F32)],
        compiler_params=_params("parallel", "arbitrary"),
        name="ffn_ln3",
    )(x32, x16, w_up, w_up, w_down, g.reshape(1, d), b.reshape(1, d))


def kernel(x, mem, w_in, lam_qk, diff_subln, conv_w, conv_b, gate_w, gate_b, lru_lambda, w_br_a, w_br_b, w_br_c, w_out, ln1_g, ln1_b, xq, xkv, xo, ln2_g, ln2_b, w_up, w_down, ln3_g, ln3_b):
    bsz, seq, d = x.shape
    n = bsz * seq
    tables = _rotary_tables(seq)
    mem16 = mem.reshape(bsz * mem.shape[1], d).astype(_BF16)
    x32 = x.reshape(n, d)
    x16 = x32.astype(_BF16)
    for l in range(DEPTH):
        w = w_in[l].astype(_BF16)
        lambda_init = 0.8 - 0.6 * math.exp(-0.3 * l)
        qkv_a = _project("proj_qkv_a", x16, w, COL_QK_A, 1536, _BF16, seq, tables, rotary_width=1024)
        rxg = _project("proj_rnn", x16, w, COL_RNN, 2048, _F32, seq)
        qkv_c = _project("proj_qkv_c", x16, w, COL_QK_C, 1536, _BF16, seq, tables, rotary_width=1024)
        w_gates = w[:, COL_GATE:COL_GATE + 3 * d].reshape(d, 3, d).transpose(1, 0, 2)

        o_a = _diff_attention(qkv_a.reshape(bsz, seq, 1536), lam_qk[l], diff_subln[l], lambda_init)
        o_b = _rg_lru(rxg.reshape(bsz, seq, 2048), conv_w[l], conv_b[l], gate_w[l], gate_b[l],
                      lru_lambda[l])
        o_c = _dilated_attention(qkv_c.reshape(bsz, seq, 1536))

        x32, x16 = _merge(o_a.reshape(n, 512), o_b.reshape(n, 1024), o_c.reshape(n, 512), x16, x32,
                          w_gates, w_br_a[l].astype(_BF16), w_br_b[l].astype(_BF16), w_br_c[l].astype(_BF16),
                          w_out[l].astype(_BF16), ln1_g[l], ln1_b[l])

        kv = _project("proj_mem_kv", mem16, xkv[l].astype(_BF16), 0, 2 * d, _BF16, mem.shape[1])
        x32, x16 = _cross_attention(x32.reshape(bsz, seq, d), x16.reshape(bsz, seq, d),
                                    kv.reshape(bsz, mem.shape[1], 2 * d), xq[l].astype(_BF16),
                                    xo[l].astype(_BF16), ln2_g[l], ln2_b[l])
        x32, x16 = _ffn(x32.reshape(n, d), x16.reshape(n, d), w_up[l].astype(_BF16),
                        w_down[l].astype(_BF16), ln3_g[l], ln3_b[l])
    return x32.reshape(bsz, seq, d)
```

```python
import functools
import math

import jax
import jax.numpy as jnp
from jax import lax
from jax.experimental import pallas as pl
from jax.experimental.pallas import tpu as pltpu

D_MODEL = 1024
DEPTH = 2
HEAD_DIM = 64
ROT_DIM = HEAD_DIM // 4
ROPE_THETA = 500000.0
DIFF_HEADS = 4
DIFF_WIDTH = 512
DIL_HEADS = 8
DIL_WIDTH = 512
DIL_STEPS = 128
DILATIONS = (1, 4, 16)
RNN_WIDTH = D_MODEL
LRU_BLOCK = 64
LRU_C = 8.0
CONV_WIDTH = 4
MEM_TOKENS = 256
XATTN_HEADS = 4
XATTN_HEAD_DIM = D_MODEL // XATTN_HEADS
FFN_HIDDEN = 2816
DEEPNORM_ALPHA = (2 * DEPTH) ** 0.25
LN_EPS = 1e-5

COL_QK_A = 0
COL_RNN = 1536
COL_QK_C = 3584
COL_GATE = 5120

LANES = 128
GATE_TILE = 256
V7X_VMEM_BYTES = 64 * 1024 * 1024
VMEM_LIMIT = V7X_VMEM_BYTES * 3 // 4

_BF16 = jnp.bfloat16
_F32 = jnp.float32
_NT = (((1,), (1,)), ((), ()))


def _params(*semantics):
    return pltpu.CompilerParams(dimension_semantics=semantics, vmem_limit_bytes=VMEM_LIMIT)


def _layer_norm(z, g, b):
    mu = jnp.mean(z, axis=-1, keepdims=True)
    d = z - mu
    var = jnp.mean(d * d, axis=-1, keepdims=True)
    return d * lax.rsqrt(var + LN_EPS) * g + b


def _proj_kernel(x_ref, w_ref, *rest, rotary_blocks):
    acc = jnp.dot(x_ref[...], w_ref[...], preferred_element_type=_F32)
    o_ref = rest[-1]
    slabs = len(o_ref.shape) == 3

    def store(j, val):
        if slabs:
            o_ref[j] = val.astype(o_ref.dtype)
        else:
            o_ref[:, j * LANES:(j + 1) * LANES] = val.astype(o_ref.dtype)

    def plain():
        for j in range(acc.shape[1] // LANES):
            store(j, acc[:, j * LANES:(j + 1) * LANES])

    if not rotary_blocks:
        plain()
        return
    cos_ref, sdn_ref, sup_ref, _ = rest
    col_block = pl.program_id(0)
    pl.when(col_block >= rotary_blocks)(plain)

    @pl.when(col_block < rotary_blocks)
    def _rotary():
        c, sdn, sup = cos_ref[...], sdn_ref[...], sup_ref[...]
        half = ROT_DIM // 2
        for j in range(acc.shape[1] // LANES):
            blk = acc[:, j * LANES:(j + 1) * LANES]
            dn = pltpu.roll(blk, LANES - half, axis=1)
            up = pltpu.roll(blk, half, axis=1)
            store(j, blk * c + dn * sdn + up * sup)


def _rotary_tables(seq):
    half = ROT_DIM // 2
    inv_freq = jnp.power(ROPE_THETA, -2.0 * jnp.arange(half, dtype=_F32) / ROT_DIM)
    ang = jnp.arange(seq).astype(_F32)[:, None] * inv_freq[None, :]
    cos, sin = jnp.cos(ang), jnp.sin(ang)
    ones = jnp.ones((seq, HEAD_DIM - ROT_DIM), _F32)
    zeros = jnp.zeros((seq, HEAD_DIM - ROT_DIM), _F32)
    zh = jnp.zeros((seq, half), _F32)
    c = jnp.concatenate([cos, cos, ones], axis=1)
    sdn = jnp.concatenate([-sin, zh, zeros], axis=1)
    sup = jnp.concatenate([zh, sin, zeros], axis=1)
    reps = LANES // HEAD_DIM
    return tuple(jnp.tile(t, (1, reps)) for t in (c, sdn, sup))


def _project(name, x2, w, col0, width, out_dtype, seq, tables=None, rotary_width=0, slabs=False,
             tm=1024, tn=512):
    n, k = x2.shape
    tm = min(tm, seq)
    tn = min(tn, width)
    assert n % tm == 0 and width % tn == 0 and col0 % tn == 0 and seq % tm == 0
    assert rotary_width % tn == 0 and (tables is None) == (rotary_width == 0)
    grid = (width // tn, n // tm)
    in_specs = [pl.BlockSpec((tm, k), lambda j, i: (i, 0)),
                pl.BlockSpec((k, tn), lambda j, i: (0, col0 // tn + j))]
    args = [x2, w]
    if tables is not None:
        tspec = pl.BlockSpec((tm, LANES), lambda j, i: (i % (seq // tm), 0))
        in_specs += [tspec] * 3
        args += list(tables)
    if slabs:
        out_shape = jax.ShapeDtypeStruct((width // LANES, n, LANES), out_dtype)
        out_spec = pl.BlockSpec((tn // LANES, tm, LANES), lambda j, i: (j, i, 0))
    else:
        out_shape = jax.ShapeDtypeStruct((n, width), out_dtype)
        out_spec = pl.BlockSpec((tm, tn), lambda j, i: (i, j))
    return pl.pallas_call(
        functools.partial(_proj_kernel, rotary_blocks=rotary_width // tn),
        out_shape=out_shape,
        grid=grid,
        in_specs=in_specs,
        out_specs=out_spec,
        compiler_params=_params("parallel", "parallel"),
        name=name,
    )(*args)


ONES_ROWS = 16
ATTN_SUB = 256


def _column_max(x):
    rows = x.shape[0]
    while rows > 8 and rows % 2 == 0:
        rows //= 2
        x = jnp.maximum(x[:rows], x[rows:])
    return jnp.max(x, axis=0, keepdims=True)


def _diff_attn_kernel(lam_ref, g_ref, q_ref, k_ref, v_ref, o_ref, wq_scr, vt_scr, acc_scr, m_scr,
                      sa_scr, sb_scr, ma_scr, mb_scr, *, lambda_init):
    qi = pl.program_id(2)
    t = q_ref.shape[1]
    n_kv = k_ref.shape[1] // t
    vd = 2 * HEAD_DIM

    @pl.when(qi == 0)
    def _transpose_v():
        for j in range(n_kv):
            vt_scr[j, 0:vd, :] = v_ref[0, j * t:(j + 1) * t, :].astype(_F32).T.astype(_BF16)
            vt_scr[j, vd:vd + ONES_ROWS, :] = jnp.ones((ONES_ROWS, t), _BF16)

    q_t = (q_ref[0].astype(_F32) * (HEAD_DIM ** -0.5)).T
    zero = jnp.zeros((HEAD_DIM, t), _F32)
    wq_scr[:, 0:t] = jnp.concatenate([q_t[0:HEAD_DIM], zero], axis=0).astype(_BF16)
    wq_scr[:, t:2 * t] = jnp.concatenate([zero, q_t[HEAD_DIM:vd]], axis=0).astype(_BF16)
    acc_scr[...] = jnp.zeros_like(acc_scr)
    m_scr[...] = jnp.full_like(m_scr, -jnp.inf)

    chains = [slice(c * ATTN_SUB, (c + 1) * ATTN_SUB) for c in range(2 * t // ATTN_SUB)]

    def scores(j, buf, cols):
        s_ref, max_ref = buf
        k_t = k_ref[0, pl.ds(pl.multiple_of(j * t, t), t), :]
        s = jnp.dot(k_t, wq_scr[:, cols], preferred_element_type=_F32)
        s_ref[:, cols] = s
        max_ref[:, cols] = _column_max(s)

    def softmax_pv(j, buf, cols, masked):
        s_ref, max_ref = buf
        s = s_ref[:, cols]
        if masked:
            key = lax.broadcasted_iota(jnp.int32, (t, ATTN_SUB), 0)
            qry = lax.broadcasted_iota(jnp.int32, (t, ATTN_SUB), 1) + cols.start % t
            s = jnp.where(key <= qry, s, -jnp.inf)
            m_tile = _column_max(s)
        else:
            m_tile = max_ref[:, cols]
        m_prev = m_scr[:, cols]
        m_new = jnp.maximum(m_prev, m_tile)
        alpha = jnp.exp(m_prev - m_new)
        p = jnp.exp(s - m_new).astype(_BF16)
        acc_scr[:, cols] = alpha * acc_scr[:, cols] + jnp.dot(vt_scr[j], p, preferred_element_type=_F32)
        m_scr[:, cols] = m_new

    def stage(j_next, s_next, j, s_cur, masked):
        for cols in chains:
            if j_next is not None:
                scores(j_next, s_next, cols)
            softmax_pv(j, s_cur, cols, masked)

    buf_a = (sa_scr, ma_scr)
    buf_b = (sb_scr, mb_scr)
    for cols in chains:
        scores(0, buf_a, cols)

    def body(jj, carry):
        j = 2 * jj
        stage(j + 1, buf_b, j, buf_a, False)
        stage(j + 2, buf_a, j + 1, buf_b, False)
        return carry

    lax.fori_loop(0, qi // 2, body, 0)

    @pl.when(qi % 2 == 0)
    def _even_tail():
        stage(None, None, qi, buf_a, True)

    @pl.when(qi % 2 == 1)
    def _odd_tail():
        stage(qi, buf_b, qi - 1, buf_a, False)
        stage(None, None, qi, buf_b, True)

    lq = lam_ref[...]
    lam = (jnp.exp(jnp.sum(lq[0:1] * lq[1:2], axis=1, keepdims=True))
           - jnp.exp(jnp.sum(lq[2:3] * lq[3:4], axis=1, keepdims=True)) + lambda_init)
    o1 = acc_scr[0:vd, 0:t] / acc_scr[vd:vd + 1, 0:t]
    o2 = acc_scr[0:vd, t:2 * t] / acc_scr[vd:vd + 1, t:2 * t]
    o = (o1 - lam * o2).T
    y = o * lax.rsqrt(jnp.mean(o * o, axis=1, keepdims=True) + LN_EPS) * g_ref[...]
    o_ref[0] = (y * (1.0 - lambda_init)).astype(o_ref.dtype)


def _diff_attention(qkv, lam_qk, subln, lambda_init, tile=512):
    _, b, s, vd = qkv.shape
    t = min(tile, s)
    assert s % t == 0 and (2 * t) % ATTN_SUB == 0 and t % ATTN_SUB == 0 and vd == 2 * HEAD_DIM
    k_off = DIFF_WIDTH // vd
    return pl.pallas_call(
        functools.partial(_diff_attn_kernel, lambda_init=lambda_init),
        out_shape=jax.ShapeDtypeStruct((b, s, DIFF_WIDTH), _BF16),
        grid=(b, DIFF_HEADS, s // t),
        in_specs=[
            pl.BlockSpec((4, HEAD_DIM), lambda bi, h, qi: (0, 0)),
            pl.BlockSpec((1, vd), lambda bi, h, qi: (0, 0)),
            pl.BlockSpec((None, 1, t, vd), lambda bi, h, qi: (h, bi, qi, 0)),
            pl.BlockSpec((None, 1, s, vd), lambda bi, h, qi: (k_off + h, bi, 0, 0)),
            pl.BlockSpec((None, 1, s, vd), lambda bi, h, qi: (2 * k_off + h, bi, 0, 0)),
        ],
        out_specs=pl.BlockSpec((1, t, vd), lambda bi, h, qi: (bi, qi, h)),
        scratch_shapes=[pltpu.VMEM((vd, 2 * t), _BF16),
                        pltpu.VMEM((s // t, vd + ONES_ROWS, t), _BF16),
                        pltpu.VMEM((vd + ONES_ROWS, 2 * t), _F32),
                        pltpu.VMEM((1, 2 * t), _F32),
                        pltpu.VMEM((t, 2 * t), _F32),
                        pltpu.VMEM((t, 2 * t), _F32),
                        pltpu.VMEM((1, 2 * t), _F32),
                        pltpu.VMEM((1, 2 * t), _F32)],
        compiler_params=_params("parallel", "parallel", "arbitrary"),
        name="diff_attn",
    )(lam_qk, subln.reshape(1, vd), qkv, qkv, qkv)


HIST = 8


def _lru_kernel(x_ref, wx_ref, wg_ref, cw_ref, cb_ref, wbd_ref, gb_ref, lam_ref, o_ref,
                xbuf, a_buf, u_buf, h_ref):
    t = o_ref.shape[1]
    w = RNN_WIDTH

    @pl.when(pl.program_id(1) == 0)
    def _init():
        xbuf[0:HIST, :] = jnp.zeros((HIST, w), _F32)
        h_ref[...] = jnp.zeros_like(h_ref)

    x16 = x_ref[0]
    xbuf[HIST:HIST + t, :] = jnp.dot(x16, wx_ref[...], preferred_element_type=_F32)
    xc = cb_ref[...]
    for tap in range(CONV_WIDTH):
        start = HIST - (CONV_WIDTH - 1) + tap
        xc = xc + xbuf[start:start + t, :] * cw_ref[tap:tap + 1, :]
    xbuf[0:HIST, :] = xbuf[t:t + HIST, :]

    xcb = xc.astype(_BF16)
    gates = []
    for g in range(2):
        parts = [jnp.dot(xcb[:, j * GATE_TILE:(j + 1) * GATE_TILE], wbd_ref[g, j],
                         preferred_element_type=_F32) for j in range(w // GATE_TILE)]
        gates.append(jnp.concatenate(parts, axis=1) + gb_ref[g:g + 1, :])
    r = jax.nn.sigmoid(gates[0])
    i = jax.nn.sigmoid(gates[1])
    neg = -lam_ref[...]
    softplus = jnp.maximum(neg, 0.0) + jnp.log1p(jnp.exp(-jnp.abs(neg)))
    log_a = -LRU_C * r * softplus
    a = jnp.exp(log_a)
    a_buf[...] = a
    u_buf[...] = jnp.sqrt(-jnp.tanh(log_a) * (1.0 + a * a)) * (i * xc)

    def body(row, h):
        h = a_buf[pl.ds(row, 1), :] * h + u_buf[pl.ds(row, 1), :]
        u_buf[pl.ds(row, 1), :] = h
        return h

    h_ref[...] = lax.fori_loop(0, t, body, h_ref[...], unroll=8)
    gelu_gate = jax.nn.gelu(jnp.dot(x16, wg_ref[...], preferred_element_type=_F32))
    o_ref[0] = (u_buf[...] * gelu_gate).astype(o_ref.dtype)


def _block_diag_gates(gate_w):
    per = GATE_TILE // LRU_BLOCK
    n_tiles = RNN_WIDTH // GATE_TILE
    gw = gate_w.reshape(2, n_tiles, per, LRU_BLOCK, LRU_BLOCK)
    eye = jnp.eye(per, dtype=gate_w.dtype)
    dense = jnp.einsum("gtpcd,pq->gtpcqd", gw, eye)
    return dense.reshape(2, n_tiles, GATE_TILE, GATE_TILE)


def _rg_lru(x16, w_rnn, conv_w, conv_b, gate_w, gate_b, lru_lambda, tile=512):
    b, s, d = x16.shape
    t = min(tile, s)
    assert s % t == 0
    w = RNN_WIDTH
    full = lambda shape: pl.BlockSpec(shape, lambda bi, ti: (0,) * len(shape))
    return pl.pallas_call(
        _lru_kernel,
        out_shape=jax.ShapeDtypeStruct((b, s, w), _BF16),
        grid=(b, s // t),
        in_specs=[pl.BlockSpec((1, t, d), lambda bi, ti: (bi, ti, 0)),
                  pl.BlockSpec((d, w), lambda bi, ti: (0, 0)),
                  pl.BlockSpec((d, w), lambda bi, ti: (0, 1)),
                  full((CONV_WIDTH, w)), full((1, w)),
                  full((2, w // GATE_TILE, GATE_TILE, GATE_TILE)), full((2, w)), full((1, w))],
        out_specs=pl.BlockSpec((1, t, w), lambda bi, ti: (bi, ti, 0)),
        scratch_shapes=[pltpu.VMEM((t + HIST, w), _F32), pltpu.VMEM((t, w), _F32),
                        pltpu.VMEM((t, w), _F32), pltpu.VMEM((1, w), _F32)],
        compiler_params=_params("parallel", "arbitrary"),
        name="rg_lru",
    )(x16, w_rnn, w_rnn, conv_w, conv_b.reshape(1, w), _block_diag_gates(gate_w).astype(_BF16), gate_b,
      lru_lambda.reshape(1, w))


DIL_TILE = DILATIONS[-1] * DIL_STEPS
DIL_UNROLL = 4


def _rows(start, size, stride):
    return pl.ds(start, size) if stride == 1 else pl.ds(start, size, stride=stride)


def _dil_kernel(q_ref, k_ref, v_ref, o_ref, qf, kf, vf, o_run, l_run):
    i = pl.program_id(1)
    t = DIL_TILE
    st = DIL_STEPS
    hd = HEAD_DIM

    pairs = DIL_HEADS // 2

    @pl.when(i == 0)
    def _no_history():
        kf[:, 0:t, :] = jnp.zeros((pairs, t, LANES), _F32)
        vf[:, 0:t, :] = jnp.zeros((pairs, t, LANES), _F32)

    @pl.when(i > 0)
    def _shift_history():
        kf[:, 0:t, :] = kf[:, t:2 * t, :]
        vf[:, 0:t, :] = vf[:, t:2 * t, :]

    for pair in range(pairs):
        lanes = slice(pair * LANES, (pair + 1) * LANES)
        qf[pair] = q_ref[0, :, lanes].astype(_F32) * (hd ** -0.5)
        kf[pair, t:2 * t, :] = k_ref[0, :, lanes].astype(_F32)
        vf[pair, t:2 * t, :] = v_ref[0, :, lanes].astype(_F32)

    key = lax.broadcasted_iota(jnp.int32, (2 * st, 2 * st), 0)
    qry = lax.broadcasted_iota(jnp.int32, (2 * st, 2 * st), 1) % st
    dist = st + qry - key
    band = (dist >= 0) & (dist <= st)
    own_block = key >= st
    zero_q = jnp.zeros((hd, st), _F32)
    ones = jnp.ones((ONES_ROWS, 2 * st), _F32)

    def block(idx, dil, merge):
        phase = idx % dil
        n = idx // dil
        start = phase + dil * st * n
        q_rows = _rows(start, st, dil)
        kv_rows = _rows(t + start - dil * st, 2 * st, dil)
        valid = band & (own_block | (i > 0) | (n > 0))
        scores = []
        for pair in range(pairs):
            q_t = qf[pair, q_rows, :].T
            k_b = kf[pair, kv_rows, :].astype(_BF16)
            w = jnp.concatenate(
                [jnp.concatenate([q_t[0:hd], zero_q], axis=1),
                 jnp.concatenate([zero_q, q_t[hd:2 * hd]], axis=1)], axis=0).astype(_BF16)
            scores.append(jnp.dot(k_b, w, preferred_element_type=_F32))
        o_parts, lse_rows = [], []
        for pair in range(pairs):
            v_t = vf[pair, kv_rows, :].T
            s = jnp.where(valid, scores[pair], -jnp.inf)
            m = _column_max(s)
            p = jnp.exp(s - m).astype(_BF16)
            lhs = jnp.concatenate([v_t[0:hd], ones, v_t[hd:2 * hd], ones], axis=0).astype(_BF16)
            pv = jnp.dot(lhs, p, preferred_element_type=_F32)
            for hh in range(2):
                rr = (hd + ONES_ROWS) * hh
                cc = slice(st * hh, st * (hh + 1))
                l = pv[rr + hd:rr + hd + 1, cc]
                o_parts.append(pv[rr:rr + hd, cc] / l)
                lse_rows.append(m[:, cc] + jnp.log(l))
        lse = jnp.concatenate(lse_rows, axis=0)
        if merge:
            run_l = l_run[q_rows, :].T[0:DIL_HEADS]
            mx = jnp.maximum(run_l, lse)
            w_run = jnp.exp(run_l - mx)
            w_cur = jnp.exp(lse - mx)
            den = w_run + w_cur
            a = w_run / den
            b = w_cur / den
            lse = mx + jnp.log(den)
        outs = []
        for pair in range(pairs):
            halves = o_parts[2 * pair:2 * pair + 2]
            if merge:
                run_o_t = o_run[pair, q_rows, :].T
                halves = [a[h:h + 1] * run_o_t[hh * hd:(hh + 1) * hd] + b[h:h + 1] * halves[hh]
                          for hh, h in enumerate((2 * pair, 2 * pair + 1))]
            outs.append(jnp.concatenate(halves, axis=0).T)
        pad = jnp.zeros((LANES - DIL_HEADS, st), _F32)
        return q_rows, outs, jnp.concatenate([lse, pad], axis=0).T

    def store(q_rows, outs, lse):
        for pair in range(pairs):
            o_run[pair, q_rows, :] = outs[pair]
        l_run[q_rows, :] = lse

    for idx_p, dil in enumerate(reversed(DILATIONS)):
        def body(trip, carry, dil=dil, merge=idx_p > 0):
            results = [block(DIL_UNROLL * trip + u, dil, merge) for u in range(DIL_UNROLL)]
            for res in results:
                store(*res)
            return carry
        lax.fori_loop(0, t // st // DIL_UNROLL, body, 0)

    for pair in range(pairs):
        o_ref[0, :, pair * LANES:(pair + 1) * LANES] = o_run[pair].astype(o_ref.dtype)


def _dilated_attention(qkv):
    b, s, _ = qkv.shape
    t = DIL_TILE
    assert s % t == 0
    return pl.pallas_call(
        _dil_kernel,
        out_shape=jax.ShapeDtypeStruct((b, s, DIL_WIDTH), _BF16),
        grid=(b, s // t),
        in_specs=[pl.BlockSpec((1, t, DIL_WIDTH), lambda bi, i: (bi, i, 0)),
                  pl.BlockSpec((1, t, DIL_WIDTH), lambda bi, i: (bi, i, 1)),
                  pl.BlockSpec((1, t, DIL_WIDTH), lambda bi, i: (bi, i, 2))],
        out_specs=pl.BlockSpec((1, t, DIL_WIDTH), lambda bi, i: (bi, i, 0)),
        scratch_shapes=[pltpu.VMEM((DIL_WIDTH // LANES, t, LANES), _F32),
                        pltpu.VMEM((DIL_WIDTH // LANES, 2 * t, LANES), _F32),
                        pltpu.VMEM((DIL_WIDTH // LANES, 2 * t, LANES), _F32),
                        pltpu.VMEM((DIL_WIDTH // LANES, t, LANES), _F32),
                        pltpu.VMEM((t, LANES), _F32)],
        compiler_params=_params("parallel", "arbitrary"),
        name="dil_attn",
    )(qkv, qkv, qkv)


def _merge_kernel(oa_ref, ob_ref, oc_ref, x16_ref, x_ref, wg_ref, wa_ref, wb_ref, wc_ref, wo_ref,
                  g_ref, b_ref, y32_ref, y16_ref):
    merged = None
    x16 = x16_ref[...]
    for idx, (o_ref, w_ref) in enumerate(((oa_ref, wa_ref), (ob_ref, wb_ref), (oc_ref, wc_ref))):
        gate = jnp.dot(x16, wg_ref[idx], preferred_element_type=_F32)
        branch = jnp.dot(o_ref[...], w_ref[...], preferred_element_type=_F32)
        term = jax.nn.sigmoid(gate) * branch
        merged = term if merged is None else merged + term
    h = jnp.dot(merged.astype(_BF16), wo_ref[...], preferred_element_type=_F32)
    y = _layer_norm(DEEPNORM_ALPHA * x_ref[...] + h, g_ref[...], b_ref[...])
    y32_ref[...] = y
    y16_ref[...] = y.astype(_BF16)


def _merge(o_a, o_b, o_c, x16, x, w_gates, w_a, w_b, w_c, w_o, g, b, tm=256):
    n, d = x.shape
    tm = min(tm, n)
    assert n % tm == 0
    row = lambda width: pl.BlockSpec((tm, width), lambda i: (i, 0))
    full = lambda shape: pl.BlockSpec(shape, lambda i: (0,) * len(shape))
    return pl.pallas_call(
        _merge_kernel,
        out_shape=(jax.ShapeDtypeStruct((n, d), _F32), jax.ShapeDtypeStruct((n, d), _BF16)),
        grid=(n // tm,),
        in_specs=[row(DIFF_WIDTH), row(RNN_WIDTH), row(DIL_WIDTH), row(d), row(d),
                  full((3, d, d)), full((DIFF_WIDTH, d)), full((RNN_WIDTH, d)), full((DIL_WIDTH, d)),
                  full((d, d)), full((1, d)), full((1, d))],
        out_specs=(row(d), row(d)),
        compiler_params=_params("parallel"),
        name="merge_ln1",
    )(o_a, o_b, o_c, x16, x, w_gates, w_a, w_b, w_c, w_o, g.reshape(1, d), b.reshape(1, d))


def _xattn_kernel(x32_ref, x16_ref, kv_ref, wq_ref, wo_ref, g_ref, b_ref, y32_ref, y16_ref):
    d = D_MODEL
    hd = XATTN_HEAD_DIM
    q = jnp.dot(x16_ref[0], wq_ref[...], preferred_element_type=_F32)
    q = (q * (hd ** -0.5)).astype(_BF16)
    heads = []
    for h in range(XATTN_HEADS):
        k_h = kv_ref[0, :, h * hd:(h + 1) * hd]
        v_h = kv_ref[0, :, d + h * hd:d + (h + 1) * hd]
        s = lax.dot_general(q[:, h * hd:(h + 1) * hd], k_h, _NT, preferred_element_type=_F32)
        p = jnp.exp(s - jnp.max(s, axis=1, keepdims=True))
        l = jnp.sum(p, axis=1, keepdims=True)
        heads.append(jnp.dot(p.astype(_BF16), v_h, preferred_element_type=_F32) / l)
    o = jnp.concatenate(heads, axis=1).astype(_BF16)
    h_out = jnp.dot(o, wo_ref[...], preferred_element_type=_F32)
    y = _layer_norm(DEEPNORM_ALPHA * x32_ref[0] + h_out, g_ref[...], b_ref[...])
    y32_ref[0] = y
    y16_ref[0] = y.astype(_BF16)


def _cross_attention(x32, x16, kv, wq, wo, g, b, tm=512):
    bsz, s, d = x32.shape
    tm = min(tm, s)
    assert s % tm == 0
    row = pl.BlockSpec((1, tm, d), lambda bi, i: (bi, i, 0))
    full = lambda shape: pl.BlockSpec(shape, lambda bi, i: (0, 0))
    return pl.pallas_call(
        _xattn_kernel,
        out_shape=(jax.ShapeDtypeStruct((bsz, s, d), _F32), jax.ShapeDtypeStruct((bsz, s, d), _BF16)),
        grid=(bsz, s // tm),
        in_specs=[row, row, pl.BlockSpec((1, MEM_TOKENS, 2 * d), lambda bi, i: (bi, 0, 0)),
                  full((d, d)), full((d, d)), full((1, d)), full((1, d))],
        out_specs=(row, row),
        compiler_params=_params("parallel", "parallel"),
        name="xattn_ln2",
    )(x32, x16, kv, wq, wo, g.reshape(1, d), b.reshape(1, d))


def _ffn_kernel(x32_ref, x16_ref, wg_ref, wu_ref, wd_ref, g_ref, b_ref, y32_ref, y16_ref, acc_ref):
    c = pl.program_id(1)

    @pl.when(c == 0)
    def _init():
        acc_ref[...] = jnp.zeros_like(acc_ref)

    x = x16_ref[...]
    gate = jnp.dot(x, wg_ref[...], preferred_element_type=_F32)
    up = jnp.dot(x, wu_ref[...], preferred_element_type=_F32)
    hidden = (jax.nn.silu(gate) * up).astype(_BF16)
    acc_ref[...] += jnp.dot(hidden, wd_ref[...], preferred_element_type=_F32)

    @pl.when(c == pl.num_programs(1) - 1)
    def _fin():
        y = _layer_norm(DEEPNORM_ALPHA * x32_ref[...] + acc_ref[...], g_ref[...], b_ref[...])
        y32_ref[...] = y
        y16_ref[...] = y.astype(_BF16)


def _ffn(x32, x16, w_up, w_down, g, b, tm=512, chunks=2):
    n, d = x32.shape
    tm = min(tm, n)
    th = FFN_HIDDEN // chunks
    assert n % tm == 0 and FFN_HIDDEN % chunks == 0 and th % LANES == 0
    row = pl.BlockSpec((tm, d), lambda i, c: (i, 0))
    full = pl.BlockSpec((1, d), lambda i, c: (0, 0))
    return pl.pallas_call(
        _ffn_kernel,
        out_shape=(jax.ShapeDtypeStruct((n, d), _F32), jax.ShapeDtypeStruct((n, d), _BF16)),
        grid=(n // tm, chunks),
        in_specs=[row, row,
                  pl.BlockSpec((d, th), lambda i, c: (0, c)),
                  pl.BlockSpec((d, th), lambda i, c: (0, chunks + c)),
                  pl.BlockSpec((th, d), lambda i, c: (c, 0)),
                  full, full],
        out_specs=(row, row),
        scratch_shapes=[pltpu.VMEM((tm, d), _F32)],
        compiler_params=_params("parallel", "arbitrary"),
        name="ffn_ln3",
    )(x32, x16, w_up, w_up, w_down, g.reshape(1, d), b.reshape(1, d))


def kernel(x, mem, w_in, lam_qk, diff_subln, conv_w, conv_b, gate_w, gate_b, lru_lambda, w_br_a, w_br_b, w_br_c, w_out, ln1_g, ln1_b, xq, xkv, xo, ln2_g, ln2_b, w_up, w_down, ln3_g, ln3_b):
    bsz, seq, d = x.shape
    n = bsz * seq
    tables = _rotary_tables(seq)
    mem16 = mem.reshape(bsz * mem.shape[1], d).astype(_BF16)
    x32 = x.reshape(n, d)
    x16 = x32.astype(_BF16)
    for l in range(DEPTH):
        w = w_in[l].astype(_BF16)
        lambda_init = 0.8 - 0.6 * math.exp(-0.3 * l)
        qkv_a = _project("proj_qkv_a", x16, w, COL_QK_A, 1536, _BF16, seq, tables, rotary_width=1024,
                         slabs=True)
        qkv_c = _project("proj_qkv_c", x16, w, COL_QK_C, 1536, _BF16, seq, tables, rotary_width=1024)
        w_gates = w[:, COL_GATE:COL_GATE + 3 * d].reshape(d, 3, d).transpose(1, 0, 2)

        o_a = _diff_attention(qkv_a.reshape(-1, bsz, seq, LANES), lam_qk[l], diff_subln[l], lambda_init)
        o_b = _rg_lru(x16.reshape(bsz, seq, d), w[:, COL_RNN:COL_RNN + 2 * RNN_WIDTH], conv_w[l], conv_b[l],
                      gate_w[l], gate_b[l], lru_lambda[l])
        o_c = _dilated_attention(qkv_c.reshape(bsz, seq, 1536))

        x32, x16 = _merge(o_a.reshape(n, 512), o_b.reshape(n, 1024), o_c.reshape(n, 512), x16, x32,
                          w_gates, w_br_a[l].astype(_BF16), w_br_b[l].astype(_BF16), w_br_c[l].astype(_BF16),
                          w_out[l].astype(_BF16), ln1_g[l], ln1_b[l])

        kv = _project("proj_mem_kv", mem16, xkv[l].astype(_BF16), 0, 2 * d, _BF16, mem.shape[1])
        x32, x16 = _cross_attention(x32.reshape(bsz, seq, d), x16.reshape(bsz, seq, d),
                                    kv.reshape(bsz, mem.shape[1], 2 * d), xq[l].astype(_BF16),
                                    xo[l].astype(_BF16), ln2_g[l], ln2_b[l])
        x32, x16 = _ffn(x32.reshape(n, d), x16.reshape(n, d), w_up[l].astype(_BF16),
                        w_down[l].astype(_BF16), ln3_g[l], ln3_b[l])
    return x32.reshape(bsz, seq, d)
```

```python
import functools
import math

import jax
import jax.numpy as jnp
from jax import lax
from jax.experimental import pallas as pl
from jax.experimental.pallas import tpu as pltpu

D_MODEL = 1024
DEPTH = 2
HEAD_DIM = 64
ROT_DIM = HEAD_DIM // 4
ROPE_THETA = 500000.0
DIFF_HEADS = 4
DIFF_WIDTH = 512
DIL_HEADS = 8
DIL_WIDTH = 512
DIL_STEPS = 128
DILATIONS = (1, 4, 16)
RNN_WIDTH = D_MODEL
LRU_BLOCK = 64
LRU_C = 8.0
CONV_WIDTH = 4
MEM_TOKENS = 256
XATTN_HEADS = 4
XATTN_HEAD_DIM = D_MODEL // XATTN_HEADS
FFN_HIDDEN = 2816
DEEPNORM_ALPHA = (2 * DEPTH) ** 0.25
LN_EPS = 1e-5

COL_QK_A = 0
COL_RNN = 1536
COL_QK_C = 3584
COL_GATE = 5120

LANES = 128
GATE_TILE = 256
V7X_VMEM_BYTES = 64 * 1024 * 1024
VMEM_LIMIT = V7X_VMEM_BYTES * 3 // 4

_BF16 = jnp.bfloat16
_F32 = jnp.float32
_NT = (((1,), (1,)), ((), ()))


def _params(*semantics):
    return pltpu.CompilerParams(dimension_semantics=semantics, vmem_limit_bytes=VMEM_LIMIT)


def _layer_norm(z, g, b):
    mu = jnp.mean(z, axis=-1, keepdims=True)
    d = z - mu
    var = jnp.mean(d * d, axis=-1, keepdims=True)
    return d * lax.rsqrt(var + LN_EPS) * g + b


def _proj_kernel(x_ref, w_ref, *rest, rotary_blocks):
    acc = jnp.dot(x_ref[...], w_ref[...], preferred_element_type=_F32)
    o_ref = rest[-1]
    slabs = len(o_ref.shape) == 3

    def store(j, val):
        if slabs:
            o_ref[j] = val.astype(o_ref.dtype)
        else:
            o_ref[:, j * LANES:(j + 1) * LANES] = val.astype(o_ref.dtype)

    def plain():
        for j in range(acc.shape[1] // LANES):
            store(j, acc[:, j * LANES:(j + 1) * LANES])

    if not rotary_blocks:
        plain()
        return
    cos_ref, sin_ref, _ = rest
    col_block = pl.program_id(0)
    pl.when(col_block >= rotary_blocks)(plain)

    @pl.when(col_block < rotary_blocks)
    def _rotary():
        c, s = cos_ref[...], sin_ref[...]
        for j in range(acc.shape[1] // LANES):
            blk = acc[:, j * LANES:(j + 1) * LANES]
            partner = pltpu.roll(blk, LANES // 2, axis=1)
            store(j, blk * c + partner * s)


_HALF = ROT_DIM // 2
QK_LANE_ORDER = (list(range(0, _HALF)) + list(range(HEAD_DIM, HEAD_DIM + _HALF))
                 + list(range(ROT_DIM, HEAD_DIM))
                 + list(range(_HALF, ROT_DIM)) + list(range(HEAD_DIM + _HALF, HEAD_DIM + ROT_DIM))
                 + list(range(HEAD_DIM + ROT_DIM, 2 * HEAD_DIM)))


def _head0_lanes(shape, axis):
    lane = lax.broadcasted_iota(jnp.int32, shape, axis)
    return (lane < _HALF) | ((lane >= ROT_DIM) & (lane < HEAD_DIM + _HALF))


def _permute_qk_columns(w_qk):
    d, width = w_qk.shape
    grouped = w_qk.reshape(d, width // LANES, LANES)
    return jnp.take(grouped, jnp.asarray(QK_LANE_ORDER, jnp.int32), axis=2).reshape(d, width)


def _rotary_tables(seq):
    inv_freq = jnp.power(ROPE_THETA, -2.0 * jnp.arange(_HALF, dtype=_F32) / ROT_DIM)
    ang = jnp.arange(seq).astype(_F32)[:, None] * inv_freq[None, :]
    cos, sin = jnp.cos(ang), jnp.sin(ang)
    ones = jnp.ones((seq, HEAD_DIM - ROT_DIM), _F32)
    zeros = jnp.zeros((seq, HEAD_DIM - ROT_DIM), _F32)
    c = jnp.concatenate([cos, cos, ones, cos, cos, ones], axis=1)
    s = jnp.concatenate([-sin, -sin, zeros, sin, sin, zeros], axis=1)
    return c, s


def _project(name, x2, w, col0, width, out_dtype, seq, tables=None, rotary_width=0, slabs=False,
             tm=1024, tn=512):
    n, k = x2.shape
    tm = min(tm, seq)
    tn = min(tn, width)
    assert n % tm == 0 and width % tn == 0 and col0 % tn == 0 and seq % tm == 0
    assert rotary_width % tn == 0 and (tables is None) == (rotary_width == 0)
    grid = (width // tn, n // tm)
    in_specs = [pl.BlockSpec((tm, k), lambda j, i: (i, 0)),
                pl.BlockSpec((k, tn), lambda j, i: (0, col0 // tn + j))]
    args = [x2, w]
    if tables is not None:
        tspec = pl.BlockSpec((tm, LANES), lambda j, i: (i % (seq // tm), 0))
        in_specs += [tspec] * len(tables)
        args += list(tables)
    if slabs:
        out_shape = jax.ShapeDtypeStruct((width // LANES, n, LANES), out_dtype)
        out_spec = pl.BlockSpec((tn // LANES, tm, LANES), lambda j, i: (j, i, 0))
    else:
        out_shape = jax.ShapeDtypeStruct((n, width), out_dtype)
        out_spec = pl.BlockSpec((tm, tn), lambda j, i: (i, j))
    return pl.pallas_call(
        functools.partial(_proj_kernel, rotary_blocks=rotary_width // tn),
        out_shape=out_shape,
        grid=grid,
        in_specs=in_specs,
        out_specs=out_spec,
        compiler_params=_params("parallel", "parallel"),
        name=name,
    )(*args)


ONES_ROWS = 16
ATTN_SUB = 256


def _column_max(x):
    rows = x.shape[0]
    while rows > 8 and rows % 2 == 0:
        rows //= 2
        x = jnp.maximum(x[:rows], x[rows:])
    return jnp.max(x, axis=0, keepdims=True)


def _diff_attn_kernel(lam_ref, g_ref, q_ref, k_ref, v_ref, o_ref, wq_scr, vt_scr, acc_scr, m_scr,
                      sa_scr, sb_scr, ma_scr, mb_scr, *, lambda_init):
    qi = pl.program_id(2)
    t = q_ref.shape[1]
    n_kv = k_ref.shape[1] // t
    vd = 2 * HEAD_DIM

    @pl.when(qi == 0)
    def _transpose_v():
        for j in range(n_kv):
            vt_scr[j, 0:vd, :] = v_ref[0, j * t:(j + 1) * t, :].astype(_F32).T.astype(_BF16)
            vt_scr[j, vd:vd + ONES_ROWS, :] = jnp.ones((ONES_ROWS, t), _BF16)

    q_t = (q_ref[0].astype(_F32) * (HEAD_DIM ** -0.5)).T
    first = _head0_lanes(q_t.shape, 0)
    wq_scr[:, 0:t] = jnp.where(first, q_t, 0.0).astype(_BF16)
    wq_scr[:, t:2 * t] = jnp.where(first, 0.0, q_t).astype(_BF16)
    acc_scr[...] = jnp.zeros_like(acc_scr)
    m_scr[...] = jnp.full_like(m_scr, -jnp.inf)

    chains = [slice(c * ATTN_SUB, (c + 1) * ATTN_SUB) for c in range(2 * t // ATTN_SUB)]

    def scores(j, buf, cols):
        s_ref, max_ref = buf
        k_t = k_ref[0, pl.ds(pl.multiple_of(j * t, t), t), :]
        s = jnp.dot(k_t, wq_scr[:, cols], preferred_element_type=_F32)
        s_ref[:, cols] = s
        max_ref[:, cols] = _column_max(s)

    def softmax_pv(j, buf, cols, masked):
        s_ref, max_ref = buf
        s = s_ref[:, cols]
        if masked:
            key = lax.broadcasted_iota(jnp.int32, (t, ATTN_SUB), 0)
            qry = lax.broadcasted_iota(jnp.int32, (t, ATTN_SUB), 1) + cols.start % t
            s = jnp.where(key <= qry, s, -jnp.inf)
            m_tile = _column_max(s)
        else:
            m_tile = max_ref[:, cols]
        m_prev = m_scr[:, cols]
        m_new = jnp.maximum(m_prev, m_tile)
        alpha = jnp.exp(m_prev - m_new)
        p = jnp.exp(s - m_new).astype(_BF16)
        acc_scr[:, cols] = alpha * acc_scr[:, cols] + jnp.dot(vt_scr[j], p, preferred_element_type=_F32)
        m_scr[:, cols] = m_new

    def stage(j_next, s_next, j, s_cur, masked):
        for cols in chains:
            if j_next is not None:
                scores(j_next, s_next, cols)
            softmax_pv(j, s_cur, cols, masked)

    buf_a = (sa_scr, ma_scr)
    buf_b = (sb_scr, mb_scr)
    for cols in chains:
        scores(0, buf_a, cols)

    def body(jj, carry):
        j = 2 * jj
        stage(j + 1, buf_b, j, buf_a, False)
        stage(j + 2, buf_a, j + 1, buf_b, False)
        return carry

    lax.fori_loop(0, qi // 2, body, 0)

    @pl.when(qi % 2 == 0)
    def _even_tail():
        stage(None, None, qi, buf_a, True)

    @pl.when(qi % 2 == 1)
    def _odd_tail():
        stage(qi, buf_b, qi - 1, buf_a, False)
        stage(None, None, qi, buf_b, True)

    lq = lam_ref[...]
    lam = (jnp.exp(jnp.sum(lq[0:1] * lq[1:2], axis=1, keepdims=True))
           - jnp.exp(jnp.sum(lq[2:3] * lq[3:4], axis=1, keepdims=True)) + lambda_init)
    o1 = acc_scr[0:vd, 0:t] / acc_scr[vd:vd + 1, 0:t]
    o2 = acc_scr[0:vd, t:2 * t] / acc_scr[vd:vd + 1, t:2 * t]
    o = (o1 - lam * o2).T
    y = o * lax.rsqrt(jnp.mean(o * o, axis=1, keepdims=True) + LN_EPS) * g_ref[...]
    o_ref[0] = (y * (1.0 - lambda_init)).astype(o_ref.dtype)


def _diff_attention(qkv, lam_qk, subln, lambda_init, tile=512):
    _, b, s, vd = qkv.shape
    t = min(tile, s)
    assert s % t == 0 and (2 * t) % ATTN_SUB == 0 and t % ATTN_SUB == 0 and vd == 2 * HEAD_DIM
    k_off = DIFF_WIDTH // vd
    return pl.pallas_call(
        functools.partial(_diff_attn_kernel, lambda_init=lambda_init),
        out_shape=jax.ShapeDtypeStruct((b, s, DIFF_WIDTH), _BF16),
        grid=(b, DIFF_HEADS, s // t),
        in_specs=[
            pl.BlockSpec((4, HEAD_DIM), lambda bi, h, qi: (0, 0)),
            pl.BlockSpec((1, vd), lambda bi, h, qi: (0, 0)),
            pl.BlockSpec((None, 1, t, vd), lambda bi, h, qi: (h, bi, qi, 0)),
            pl.BlockSpec((None, 1, s, vd), lambda bi, h, qi: (k_off + h, bi, 0, 0)),
            pl.BlockSpec((None, 1, s, vd), lambda bi, h, qi: (2 * k_off + h, bi, 0, 0)),
        ],
        out_specs=pl.BlockSpec((1, t, vd), lambda bi, h, qi: (bi, qi, h)),
        scratch_shapes=[pltpu.VMEM((vd, 2 * t), _BF16),
                        pltpu.VMEM((s // t, vd + ONES_ROWS, t), _BF16),
                        pltpu.VMEM((vd + ONES_ROWS, 2 * t), _F32),
                        pltpu.VMEM((1, 2 * t), _F32),
                        pltpu.VMEM((t, 2 * t), _F32),
                        pltpu.VMEM((t, 2 * t), _F32),
                        pltpu.VMEM((1, 2 * t), _F32),
                        pltpu.VMEM((1, 2 * t), _F32)],
        compiler_params=_params("parallel", "parallel", "arbitrary"),
        name="diff_attn",
    )(lam_qk, subln.reshape(1, vd), qkv, qkv, qkv)


HIST = 8
LRU_TILE = 128


def _lru_kernel(x_ref, wx_ref, wg_ref, cw_ref, cb_ref, wbd_ref, gb_ref, lam_ref, o_ref,
                xbuf, a_tm, u_tm, g_buf, h_ref):
    nb, t, d = x_ref.shape
    w = RNN_WIDTH
    lane_blocks = w // LANES

    @pl.when(pl.program_id(0) == 0)
    def _init():
        xbuf[:, 0:HIST, :] = jnp.zeros((nb, HIST, w), _F32)
        h_ref[...] = jnp.zeros_like(h_ref)

    x_all = x_ref[...].reshape(nb * t, d)
    xr_all = jnp.dot(x_all, wx_ref[...], preferred_element_type=_F32)
    g_buf[...] = jax.nn.gelu(jnp.dot(x_all, wg_ref[...], preferred_element_type=_F32))
    neg = -lam_ref[...]
    softplus = jnp.maximum(neg, 0.0) + jnp.log1p(jnp.exp(-jnp.abs(neg)))

    for b in range(nb):
        xbuf[b, HIST:HIST + t, :] = xr_all[b * t:(b + 1) * t]
        xc = cb_ref[...]
        for tap in range(CONV_WIDTH):
            start = HIST - (CONV_WIDTH - 1) + tap
            xc = xc + xbuf[b, start:start + t, :] * cw_ref[tap:tap + 1, :]
        xbuf[b, 0:HIST, :] = xbuf[b, t:t + HIST, :]

        xcb = xc.astype(_BF16)
        gates = []
        for g in range(2):
            parts = [jnp.dot(xcb[:, j * GATE_TILE:(j + 1) * GATE_TILE], wbd_ref[g, j],
                             preferred_element_type=_F32) for j in range(w // GATE_TILE)]
            gates.append(jnp.concatenate(parts, axis=1) + gb_ref[g:g + 1, :])
        r = jax.nn.sigmoid(gates[0])
        i = jax.nn.sigmoid(gates[1])
        log_a = -LRU_C * r * softplus
        a = jnp.exp(log_a)
        u = jnp.sqrt(-jnp.tanh(log_a) * (1.0 + a * a)) * (i * xc)
        rows = pl.ds(b, t, stride=nb)
        for lb in range(lane_blocks):
            a_tm[lb, rows, :] = a[:, lb * LANES:(lb + 1) * LANES]
            u_tm[lb, rows, :] = u[:, lb * LANES:(lb + 1) * LANES]

    def step(ts, h):
        rows = pl.ds(pl.multiple_of(ts * nb, nb), nb)
        out = []
        for lb in range(lane_blocks):
            h_lb = a_tm[lb, rows, :] * h[lb] + u_tm[lb, rows, :]
            u_tm[lb, rows, :] = h_lb
            out.append(h_lb)
        return tuple(out)

    h0 = tuple(h_ref[:, lb * LANES:(lb + 1) * LANES] for lb in range(lane_blocks))
    h_end = lax.fori_loop(0, t, step, h0, unroll=8)
    for lb in range(lane_blocks):
        h_ref[:, lb * LANES:(lb + 1) * LANES] = h_end[lb]

    for b in range(nb):
        rows = pl.ds(b, t, stride=nb)
        h_b = jnp.concatenate([u_tm[lb, rows, :] for lb in range(lane_blocks)], axis=1)
        o_ref[b] = (h_b * g_buf[b * t:(b + 1) * t, :]).astype(o_ref.dtype)


def _block_diag_gates(gate_w):
    per = GATE_TILE // LRU_BLOCK
    n_tiles = RNN_WIDTH // GATE_TILE
    gw = gate_w.reshape(2, n_tiles, per, LRU_BLOCK, LRU_BLOCK)
    eye = jnp.eye(per, dtype=gate_w.dtype)
    dense = jnp.einsum("gtpcd,pq->gtpcqd", gw, eye)
    return dense.reshape(2, n_tiles, GATE_TILE, GATE_TILE)


def _rg_lru(x16, w_rnn, conv_w, conv_b, gate_w, gate_b, lru_lambda):
    b, s, d = x16.shape
    t = min(LRU_TILE, s)
    assert s % t == 0
    w = RNN_WIDTH
    full = lambda shape: pl.BlockSpec(shape, lambda ti: (0,) * len(shape))
    return pl.pallas_call(
        _lru_kernel,
        out_shape=jax.ShapeDtypeStruct((b, s, w), _BF16),
        grid=(s // t,),
        in_specs=[pl.BlockSpec((b, t, d), lambda ti: (0, ti, 0)),
                  pl.BlockSpec((d, w), lambda ti: (0, 0)),
                  pl.BlockSpec((d, w), lambda ti: (0, 1)),
                  full((CONV_WIDTH, w)), full((1, w)),
                  full((2, w // GATE_TILE, GATE_TILE, GATE_TILE)), full((2, w)), full((1, w))],
        out_specs=pl.BlockSpec((b, t, w), lambda ti: (0, ti, 0)),
        scratch_shapes=[pltpu.VMEM((b, t + HIST, w), _F32),
                        pltpu.VMEM((w // LANES, t * b, LANES), _F32),
                        pltpu.VMEM((w // LANES, t * b, LANES), _F32),
                        pltpu.VMEM((t * b, w), _F32),
                        pltpu.VMEM((b, w), _F32)],
        compiler_params=_params("arbitrary"),
        name="rg_lru",
    )(x16, w_rnn, w_rnn, conv_w, conv_b.reshape(1, w), _block_diag_gates(gate_w).astype(_BF16), gate_b,
      lru_lambda.reshape(1, w))


DIL_TILE = DILATIONS[-1] * DIL_STEPS
DIL_UNROLL = 4


def _rows(start, size, stride):
    return pl.ds(start, size) if stride == 1 else pl.ds(start, size, stride=stride)


def _dil_kernel(q_ref, k_ref, v_ref, o_ref, qf, kf, vf, o_run, l_run):
    i = pl.program_id(1)
    t = DIL_TILE
    st = DIL_STEPS
    hd = HEAD_DIM

    pairs = DIL_HEADS // 2

    @pl.when(i == 0)
    def _no_history():
        kf[:, 0:t, :] = jnp.zeros((pairs, t, LANES), _F32)
        vf[:, 0:t, :] = jnp.zeros((pairs, t, LANES), _F32)

    @pl.when(i > 0)
    def _shift_history():
        kf[:, 0:t, :] = kf[:, t:2 * t, :]
        vf[:, 0:t, :] = vf[:, t:2 * t, :]

    for pair in range(pairs):
        lanes = slice(pair * LANES, (pair + 1) * LANES)
        qf[pair] = q_ref[0, :, lanes].astype(_F32) * (hd ** -0.5)
        kf[pair, t:2 * t, :] = k_ref[0, :, lanes].astype(_F32)
        vf[pair, t:2 * t, :] = v_ref[0, :, lanes].astype(_F32)

    key = lax.broadcasted_iota(jnp.int32, (2 * st, 2 * st), 0)
    qry = lax.broadcasted_iota(jnp.int32, (2 * st, 2 * st), 1) % st
    dist = st + qry - key
    band = (dist >= 0) & (dist <= st)
    own_block = key >= st
    first_head = _head0_lanes((LANES, st), 0)
    ones = jnp.ones((ONES_ROWS, 2 * st), _F32)

    def block(idx, dil, merge):
        phase = idx % dil
        n = idx // dil
        start = phase + dil * st * n
        q_rows = _rows(start, st, dil)
        kv_rows = _rows(t + start - dil * st, 2 * st, dil)
        valid = band & (own_block | (i > 0) | (n > 0))
        scores = []
        for pair in range(pairs):
            q_t = qf[pair, q_rows, :].T
            k_b = kf[pair, kv_rows, :].astype(_BF16)
            w = jnp.concatenate([jnp.where(first_head, q_t, 0.0),
                                 jnp.where(first_head, 0.0, q_t)], axis=1).astype(_BF16)
            scores.append(jnp.dot(k_b, w, preferred_element_type=_F32))
        o_parts, lse_rows = [], []
        for pair in range(pairs):
            v_t = vf[pair, kv_rows, :].T
            s = jnp.where(valid, scores[pair], -jnp.inf)
            m = _column_max(s)
            p = jnp.exp(s - m).astype(_BF16)
            lhs = jnp.concatenate([v_t[0:hd], ones, v_t[hd:2 * hd], ones], axis=0).astype(_BF16)
            pv = jnp.dot(lhs, p, preferred_element_type=_F32)
            for hh in range(2):
                rr = (hd + ONES_ROWS) * hh
                cc = slice(st * hh, st * (hh + 1))
                l = pv[rr + hd:rr + hd + 1, cc]
                o_parts.append(pv[rr:rr + hd, cc] / l)
                lse_rows.append(m[:, cc] + jnp.log(l))
        lse = jnp.concatenate(lse_rows, axis=0)
        if merge:
            run_l = l_run[q_rows, :].T[0:DIL_HEADS]
            mx = jnp.maximum(run_l, lse)
            w_run = jnp.exp(run_l - mx)
            w_cur = jnp.exp(lse - mx)
            den = w_run + w_cur
            a = w_run / den
            b = w_cur / den
            lse = mx + jnp.log(den)
        outs = []
        for pair in range(pairs):
            halves = o_parts[2 * pair:2 * pair + 2]
            if merge:
                run_o_t = o_run[pair, q_rows, :].T
                halves = [a[h:h + 1] * run_o_t[hh * hd:(hh + 1) * hd] + b[h:h + 1] * halves[hh]
                          for hh, h in enumerate((2 * pair, 2 * pair + 1))]
            outs.append(jnp.concatenate(halves, axis=0).T)
        pad = jnp.zeros((LANES - DIL_HEADS, st), _F32)
        return q_rows, outs, jnp.concatenate([lse, pad], axis=0).T

    def store(q_rows, outs, lse):
        for pair in range(pairs):
            o_run[pair, q_rows, :] = outs[pair]
        l_run[q_rows, :] = lse

    for idx_p, dil in enumerate(reversed(DILATIONS)):
        def body(trip, carry, dil=dil, merge=idx_p > 0):
            results = [block(DIL_UNROLL * trip + u, dil, merge) for u in range(DIL_UNROLL)]
            for res in results:
                store(*res)
            return carry
        lax.fori_loop(0, t // st // DIL_UNROLL, body, 0)

    for pair in range(pairs):
        o_ref[0, :, pair * LANES:(pair + 1) * LANES] = o_run[pair].astype(o_ref.dtype)


def _dilated_attention(qkv):
    b, s, _ = qkv.shape
    t = DIL_TILE
    assert s % t == 0
    return pl.pallas_call(
        _dil_kernel,
        out_shape=jax.ShapeDtypeStruct((b, s, DIL_WIDTH), _BF16),
        grid=(b, s // t),
        in_specs=[pl.BlockSpec((1, t, DIL_WIDTH), lambda bi, i: (bi, i, 0)),
                  pl.BlockSpec((1, t, DIL_WIDTH), lambda bi, i: (bi, i, 1)),
                  pl.BlockSpec((1, t, DIL_WIDTH), lambda bi, i: (bi, i, 2))],
        out_specs=pl.BlockSpec((1, t, DIL_WIDTH), lambda bi, i: (bi, i, 0)),
        scratch_shapes=[pltpu.VMEM((DIL_WIDTH // LANES, t, LANES), _F32),
                        pltpu.VMEM((DIL_WIDTH // LANES, 2 * t, LANES), _F32),
                        pltpu.VMEM((DIL_WIDTH // LANES, 2 * t, LANES), _F32),
                        pltpu.VMEM((DIL_WIDTH // LANES, t, LANES), _F32),
                        pltpu.VMEM((t, LANES), _F32)],
        compiler_params=_params("parallel", "arbitrary"),
        name="dil_attn",
    )(qkv, qkv, qkv)


def _merge_kernel(oa_ref, ob_ref, oc_ref, x16_ref, x_ref, wg_ref, wa_ref, wb_ref, wc_ref, wo_ref,
                  g_ref, b_ref, y32_ref, y16_ref):
    merged = None
    x16 = x16_ref[...]
    for idx, (o_ref, w_ref) in enumerate(((oa_ref, wa_ref), (ob_ref, wb_ref), (oc_ref, wc_ref))):
        gate = jnp.dot(x16, wg_ref[idx], preferred_element_type=_F32)
        branch = jnp.dot(o_ref[...], w_ref[...], preferred_element_type=_F32)
        term = jax.nn.sigmoid(gate) * branch
        merged = term if merged is None else merged + term
    h = jnp.dot(merged.astype(_BF16), wo_ref[...], preferred_element_type=_F32)
    y = _layer_norm(DEEPNORM_ALPHA * x_ref[...] + h, g_ref[...], b_ref[...])
    y32_ref[...] = y
    y16_ref[...] = y.astype(_BF16)


def _merge(o_a, o_b, o_c, x16, x, w_gates, w_a, w_b, w_c, w_o, g, b, tm=256):
    n, d = x.shape
    tm = min(tm, n)
    assert n % tm == 0
    row = lambda width: pl.BlockSpec((tm, width), lambda i: (i, 0))
    full = lambda shape: pl.BlockSpec(shape, lambda i: (0,) * len(shape))
    return pl.pallas_call(
        _merge_kernel,
        out_shape=(jax.ShapeDtypeStruct((n, d), _F32), jax.ShapeDtypeStruct((n, d), _BF16)),
        grid=(n // tm,),
        in_specs=[row(DIFF_WIDTH), row(RNN_WIDTH), row(DIL_WIDTH), row(d), row(d),
                  full((3, d, d)), full((DIFF_WIDTH, d)), full((RNN_WIDTH, d)), full((DIL_WIDTH, d)),
                  full((d, d)), full((1, d)), full((1, d))],
        out_specs=(row(d), row(d)),
        compiler_params=_params("parallel"),
        name="merge_ln1",
    )(o_a, o_b, o_c, x16, x, w_gates, w_a, w_b, w_c, w_o, g.reshape(1, d), b.reshape(1, d))


def _xattn_kernel(x32_ref, x16_ref, kv_ref, wq_ref, wo_ref, g_ref, b_ref, y32_ref, y16_ref):
    d = D_MODEL
    hd = XATTN_HEAD_DIM
    q = jnp.dot(x16_ref[0], wq_ref[...], preferred_element_type=_F32)
    q = (q * (hd ** -0.5)).astype(_BF16)
    heads = []
    for h in range(XATTN_HEADS):
        k_h = kv_ref[0, :, h * hd:(h + 1) * hd]
        v_h = kv_ref[0, :, d + h * hd:d + (h + 1) * hd]
        s = lax.dot_general(q[:, h * hd:(h + 1) * hd], k_h, _NT, preferred_element_type=_F32)
        p = jnp.exp(s - jnp.max(s, axis=1, keepdims=True))
        l = jnp.sum(p, axis=1, keepdims=True)
        heads.append(jnp.dot(p.astype(_BF16), v_h, preferred_element_type=_F32) / l)
    o = jnp.concatenate(heads, axis=1).astype(_BF16)
    h_out = jnp.dot(o, wo_ref[...], preferred_element_type=_F32)
    y = _layer_norm(DEEPNORM_ALPHA * x32_ref[0] + h_out, g_ref[...], b_ref[...])
    y32_ref[0] = y
    y16_ref[0] = y.astype(_BF16)


def _cross_attention(x32, x16, kv, wq, wo, g, b, tm=512):
    bsz, s, d = x32.shape
    tm = min(tm, s)
    assert s % tm == 0
    row = pl.BlockSpec((1, tm, d), lambda bi, i: (bi, i, 0))
    full = lambda shape: pl.BlockSpec(shape, lambda bi, i: (0, 0))
    return pl.pallas_call(
        _xattn_kernel,
        out_shape=(jax.ShapeDtypeStruct((bsz, s, d), _F32), jax.ShapeDtypeStruct((bsz, s, d), _BF16)),
        grid=(bsz, s // tm),
        in_specs=[row, row, pl.BlockSpec((1, MEM_TOKENS, 2 * d), lambda bi, i: (bi, 0, 0)),
                  full((d, d)), full((d, d)), full((1, d)), full((1, d))],
        out_specs=(row, row),
        compiler_params=_params("parallel", "parallel"),
        name="xattn_ln2",
    )(x32, x16, kv, wq, wo, g.reshape(1, d), b.reshape(1, d))


def _ffn_kernel(x32_ref, x16_ref, wg_ref, wu_ref, wd_ref, g_ref, b_ref, y32_ref, y16_ref, acc_ref):
    c = pl.program_id(1)

    @pl.when(c == 0)
    def _init():
        acc_ref[...] = jnp.zeros_like(acc_ref)

    x = x16_ref[...]
    gate = jnp.dot(x, wg_ref[...], preferred_element_type=_F32)
    up = jnp.dot(x, wu_ref[...], preferred_element_type=_F32)
    hidden = (jax.nn.silu(gate) * up).astype(_BF16)
    acc_ref[...] += jnp.dot(hidden, wd_ref[...], preferred_element_type=_F32)

    @pl.when(c == pl.num_programs(1) - 1)
    def _fin():
        y = _layer_norm(DEEPNORM_ALPHA * x32_ref[...] + acc_ref[...], g_ref[...], b_ref[...])
        y32_ref[...] = y
        y16_ref[...] = y.astype(_BF16)


def _ffn(x32, x16, w_up, w_down, g, b, tm=512, chunks=2):
    n, d = x32.shape
    tm = min(tm, n)
    th = FFN_HIDDEN // chunks
    assert n % tm == 0 and FFN_HIDDEN % chunks == 0 and th % LANES == 0
    row = pl.BlockSpec((tm, d), lambda i, c: (i, 0))
    full = pl.BlockSpec((1, d), lambda i, c: (0, 0))
    return pl.pallas_call(
        _ffn_kernel,
        out_shape=(jax.ShapeDtypeStruct((n, d), _F32), jax.ShapeDtypeStruct((n, d), _BF16)),
        grid=(n // tm, chunks),
        in_specs=[row, row,
                  pl.BlockSpec((d, th), lambda i, c: (0, c)),
                  pl.BlockSpec((d, th), lambda i, c: (0, chunks + c)),
                  pl.BlockSpec((th, d), lambda i, c: (c, 0)),
                  full, full],
        out_specs=(row, row),
        scratch_shapes=[pltpu.VMEM((tm, d), _F32)],
        compiler_params=_params("parallel", "arbitrary"),
        name="ffn_ln3",
    )(x32, x16, w_up, w_up, w_down, g.reshape(1, d), b.reshape(1, d))


def kernel(x, mem, w_in, lam_qk, diff_subln, conv_w, conv_b, gate_w, gate_b, lru_lambda, w_br_a, w_br_b, w_br_c, w_out, ln1_g, ln1_b, xq, xkv, xo, ln2_g, ln2_b, w_up, w_down, ln3_g, ln3_b):
    bsz, seq, d = x.shape
    n = bsz * seq
    tables = _rotary_tables(seq)
    mem16 = mem.reshape(bsz * mem.shape[1], d).astype(_BF16)
    x32 = x.reshape(n, d)
    x16 = x32.astype(_BF16)
    for l in range(DEPTH):
        w = w_in[l].astype(_BF16)
        lambda_init = 0.8 - 0.6 * math.exp(-0.3 * l)
        w_qkv_a, w_qkv_c = (
            jnp.concatenate([_permute_qk_columns(w[:, c0:c0 + 1024]), w[:, c0 + 1024:c0 + 1536]], axis=1)
            for c0 in (COL_QK_A, COL_QK_C))
        qkv_a = _project("proj_qkv_a", x16, w_qkv_a, 0, 1536, _BF16, seq, tables, rotary_width=1024,
                         slabs=True)
        qkv_c = _project("proj_qkv_c", x16, w_qkv_c, 0, 1536, _BF16, seq, tables, rotary_width=1024)
        w_gates = w[:, COL_GATE:COL_GATE + 3 * d].reshape(d, 3, d).transpose(1, 0, 2)

        o_a = _diff_attention(qkv_a.reshape(-1, bsz, seq, LANES), lam_qk[l], diff_subln[l], lambda_init)
        o_b = _rg_lru(x16.reshape(bsz, seq, d), w[:, COL_RNN:COL_RNN + 2 * RNN_WIDTH], conv_w[l], conv_b[l],
                      gate_w[l], gate_b[l], lru_lambda[l])
        o_c = _dilated_attention(qkv_c.reshape(bsz, seq, 1536))

        x32, x16 = _merge(o_a.reshape(n, 512), o_b.reshape(n, 1024), o_c.reshape(n, 512), x16, x32,
                          w_gates, w_br_a[l].astype(_BF16), w_br_b[l].astype(_BF16), w_br_c[l].astype(_BF16),
                          w_out[l].astype(_BF16), ln1_g[l], ln1_b[l])

        kv = _project("proj_mem_kv", mem16, xkv[l].astype(_BF16), 0, 2 * d, _BF16, mem.shape[1])
        x32, x16 = _cross_attention(x32.reshape(bsz, seq, d), x16.reshape(bsz, seq, d),
                                    kv.reshape(bsz, mem.shape[1], 2 * d), xq[l].astype(_BF16),
                                    xo[l].astype(_BF16), ln2_g[l], ln2_b[l])
        x32, x16 = _ffn(x32.reshape(n, d), x16.reshape(n, d), w_up[l].astype(_BF16),
                        w_down[l].astype(_BF16), ln3_g[l], ln3_b[l])
    return x32.reshape(bsz, seq, d)
```

```python
import functools
import math

import jax
import jax.numpy as jnp
from jax import lax
from jax.experimental import pallas as pl
from jax.experimental.pallas import tpu as pltpu

D_MODEL = 1024
DEPTH = 2
HEAD_DIM = 64
ROT_DIM = HEAD_DIM // 4
ROPE_THETA = 500000.0
DIFF_HEADS = 4
DIFF_WIDTH = 512
DIL_HEADS = 8
DIL_WIDTH = 512
DIL_STEPS = 128
DILATIONS = (1, 4, 16)
RNN_WIDTH = D_MODEL
LRU_BLOCK = 64
LRU_C = 8.0
CONV_WIDTH = 4
MEM_TOKENS = 256
XATTN_HEADS = 4
XATTN_HEAD_DIM = D_MODEL // XATTN_HEADS
FFN_HIDDEN = 2816
DEEPNORM_ALPHA = (2 * DEPTH) ** 0.25
LN_EPS = 1e-5

COL_QK_A = 0
COL_RNN = 1536
COL_QK_C = 3584
COL_GATE = 5120

LANES = 128
GATE_TILE = 256
V7X_VMEM_BYTES = 64 * 1024 * 1024
VMEM_LIMIT = V7X_VMEM_BYTES * 3 // 4

_BF16 = jnp.bfloat16
_F32 = jnp.float32
_NT = (((1,), (1,)), ((), ()))


def _params(*semantics):
    return pltpu.CompilerParams(dimension_semantics=semantics, vmem_limit_bytes=VMEM_LIMIT)


def _layer_norm(z, g, b):
    mu = jnp.mean(z, axis=-1, keepdims=True)
    d = z - mu
    var = jnp.mean(d * d, axis=-1, keepdims=True)
    return d * lax.rsqrt(var + LN_EPS) * g + b


MXU_WIDTH = 256


def _proj_kernel(x_ref, w_ref, *rest, rotary_width):
    o_ref = rest[-1]
    slabs = len(o_ref.shape) == 3
    x = x_ref[...]
    width = w_ref.shape[1]
    accs = [jnp.dot(x, w_ref[:, c0:c0 + MXU_WIDTH], preferred_element_type=_F32)
            for c0 in range(0, width, MXU_WIDTH)]
    if rotary_width:
        c, s = rest[0][...], rest[1][...]
    per = MXU_WIDTH // LANES
    for g, acc in enumerate(accs):
        for jj in range(per):
            j = g * per + jj
            blk = acc[:, jj * LANES:(jj + 1) * LANES]
            if j * LANES < rotary_width:
                partner = pltpu.roll(blk, LANES // 2, axis=1)
                blk = blk * c + partner * s
            if slabs:
                o_ref[j] = blk.astype(o_ref.dtype)
            else:
                o_ref[:, j * LANES:(j + 1) * LANES] = blk.astype(o_ref.dtype)


_HALF = ROT_DIM // 2
QK_LANE_ORDER = (list(range(0, _HALF)) + list(range(HEAD_DIM, HEAD_DIM + _HALF))
                 + list(range(ROT_DIM, HEAD_DIM))
                 + list(range(_HALF, ROT_DIM)) + list(range(HEAD_DIM + _HALF, HEAD_DIM + ROT_DIM))
                 + list(range(HEAD_DIM + ROT_DIM, 2 * HEAD_DIM)))


def _head0_lanes(shape, axis):
    lane = lax.broadcasted_iota(jnp.int32, shape, axis)
    return (lane < _HALF) | ((lane >= ROT_DIM) & (lane < HEAD_DIM + _HALF))


def _permute_qk_columns(w_qk):
    d, width = w_qk.shape
    grouped = w_qk.reshape(d, width // LANES, LANES)
    return jnp.take(grouped, jnp.asarray(QK_LANE_ORDER, jnp.int32), axis=2).reshape(d, width)


def _rotary_tables(seq):
    inv_freq = jnp.power(ROPE_THETA, -2.0 * jnp.arange(_HALF, dtype=_F32) / ROT_DIM)
    ang = jnp.arange(seq).astype(_F32)[:, None] * inv_freq[None, :]
    cos, sin = jnp.cos(ang), jnp.sin(ang)
    ones = jnp.ones((seq, HEAD_DIM - ROT_DIM), _F32)
    zeros = jnp.zeros((seq, HEAD_DIM - ROT_DIM), _F32)
    c = jnp.concatenate([cos, cos, ones, cos, cos, ones], axis=1)
    s = jnp.concatenate([-sin, -sin, zeros, sin, sin, zeros], axis=1)
    return c, s


def _project(name, x2, w, out_dtype, seq, tables=None, rotary_width=0, slabs=False, tm=1024):
    n, k = x2.shape
    width = w.shape[1]
    tm = min(tm, seq)
    assert n % tm == 0 and seq % tm == 0 and width % MXU_WIDTH == 0 and rotary_width % LANES == 0
    assert (tables is None) == (rotary_width == 0)
    in_specs = [pl.BlockSpec((tm, k), lambda i: (i, 0)), pl.BlockSpec((k, width), lambda i: (0, 0))]
    args = [x2, w]
    if tables is not None:
        in_specs += [pl.BlockSpec((tm, LANES), lambda i: (i % (seq // tm), 0))] * len(tables)
        args += list(tables)
    if slabs:
        out_shape = jax.ShapeDtypeStruct((width // LANES, n, LANES), out_dtype)
        out_spec = pl.BlockSpec((width // LANES, tm, LANES), lambda i: (0, i, 0))
    else:
        out_shape = jax.ShapeDtypeStruct((n, width), out_dtype)
        out_spec = pl.BlockSpec((tm, width), lambda i: (i, 0))
    return pl.pallas_call(
        functools.partial(_proj_kernel, rotary_width=rotary_width),
        out_shape=out_shape,
        grid=(n // tm,),
        in_specs=in_specs,
        out_specs=out_spec,
        compiler_params=_params("parallel"),
        name=name,
    )(*args)


ONES_ROWS = 16
ATTN_SUB = 256


def _column_max(x):
    rows = x.shape[0]
    while rows > 8 and rows % 2 == 0:
        rows //= 2
        x = jnp.maximum(x[:rows], x[rows:])
    return jnp.max(x, axis=0, keepdims=True)


def _diff_attn_kernel(lam_ref, g_ref, q_ref, k_ref, v_ref, o_ref, wq_scr, vt_scr, acc_scr, m_scr,
                      sa_scr, sb_scr, ma_scr, mb_scr, *, lambda_init):
    qi = pl.program_id(2)
    t = q_ref.shape[1]
    n_kv = k_ref.shape[1] // t
    vd = 2 * HEAD_DIM

    @pl.when(qi == 0)
    def _transpose_v():
        for j in range(n_kv):
            vt_scr[j, 0:vd, :] = v_ref[0, j * t:(j + 1) * t, :].astype(_F32).T.astype(_BF16)
            vt_scr[j, vd:vd + ONES_ROWS, :] = jnp.ones((ONES_ROWS, t), _BF16)

    q_t = (q_ref[0].astype(_F32) * (HEAD_DIM ** -0.5)).T
    first = _head0_lanes(q_t.shape, 0)
    wq_scr[:, 0:t] = jnp.where(first, q_t, 0.0).astype(_BF16)
    wq_scr[:, t:2 * t] = jnp.where(first, 0.0, q_t).astype(_BF16)
    acc_scr[...] = jnp.zeros_like(acc_scr)
    m_scr[...] = jnp.full_like(m_scr, -jnp.inf)

    chains = [slice(c * ATTN_SUB, (c + 1) * ATTN_SUB) for c in range(2 * t // ATTN_SUB)]

    def scores(j, buf, cols):
        s_ref, max_ref = buf
        k_t = k_ref[0, pl.ds(pl.multiple_of(j * t, t), t), :]
        s = jnp.dot(k_t, wq_scr[:, cols], preferred_element_type=_F32)
        s_ref[:, cols] = s
        max_ref[:, cols] = _column_max(s)

    def softmax_pv(j, buf, cols, masked):
        s_ref, max_ref = buf
        s = s_ref[:, cols]
        if masked:
            key = lax.broadcasted_iota(jnp.int32, (t, ATTN_SUB), 0)
            qry = lax.broadcasted_iota(jnp.int32, (t, ATTN_SUB), 1) + cols.start % t
            s = jnp.where(key <= qry, s, -jnp.inf)
            m_tile = _column_max(s)
        else:
            m_tile = max_ref[:, cols]
        m_prev = m_scr[:, cols]
        m_new = jnp.maximum(m_prev, m_tile)
        alpha = jnp.exp(m_prev - m_new)
        p = jnp.exp(s - m_new).astype(_BF16)
        acc_scr[:, cols] = alpha * acc_scr[:, cols] + jnp.dot(vt_scr[j], p, preferred_element_type=_F32)
        m_scr[:, cols] = m_new

    def stage(j_next, s_next, j, s_cur, masked):
        for cols in chains:
            if j_next is not None:
                scores(j_next, s_next, cols)
            softmax_pv(j, s_cur, cols, masked)

    buf_a = (sa_scr, ma_scr)
    buf_b = (sb_scr, mb_scr)
    for cols in chains:
        scores(0, buf_a, cols)

    def body(jj, carry):
        j = 2 * jj
        stage(j + 1, buf_b, j, buf_a, False)
        stage(j + 2, buf_a, j + 1, buf_b, False)
        return carry

    lax.fori_loop(0, qi // 2, body, 0)

    @pl.when(qi % 2 == 0)
    def _even_tail():
        stage(None, None, qi, buf_a, True)

    @pl.when(qi % 2 == 1)
    def _odd_tail():
        stage(qi, buf_b, qi - 1, buf_a, False)
        stage(None, None, qi, buf_b, True)

    lq = lam_ref[...]
    lam = (jnp.exp(jnp.sum(lq[0:1] * lq[1:2], axis=1, keepdims=True))
           - jnp.exp(jnp.sum(lq[2:3] * lq[3:4], axis=1, keepdims=True)) + lambda_init)
    o1 = acc_scr[0:vd, 0:t] / acc_scr[vd:vd + 1, 0:t]
    o2 = acc_scr[0:vd, t:2 * t] / acc_scr[vd:vd + 1, t:2 * t]
    o = (o1 - lam * o2).T
    y = o * lax.rsqrt(jnp.mean(o * o, axis=1, keepdims=True) + LN_EPS) * g_ref[...]
    o_ref[0] = (y * (1.0 - lambda_init)).astype(o_ref.dtype)


def _diff_attention(qkv, lam_qk, subln, lambda_init, tile=512):
    _, b, s, vd = qkv.shape
    t = min(tile, s)
    assert s % t == 0 and (2 * t) % ATTN_SUB == 0 and t % ATTN_SUB == 0 and vd == 2 * HEAD_DIM
    k_off = DIFF_WIDTH // vd
    return pl.pallas_call(
        functools.partial(_diff_attn_kernel, lambda_init=lambda_init),
        out_shape=jax.ShapeDtypeStruct((b, s, DIFF_WIDTH), _BF16),
        grid=(b, DIFF_HEADS, s // t),
        in_specs=[
            pl.BlockSpec((4, HEAD_DIM), lambda bi, h, qi: (0, 0)),
            pl.BlockSpec((1, vd), lambda bi, h, qi: (0, 0)),
            pl.BlockSpec((None, 1, t, vd), lambda bi, h, qi: (h, bi, qi, 0)),
            pl.BlockSpec((None, 1, s, vd), lambda bi, h, qi: (k_off + h, bi, 0, 0)),
            pl.BlockSpec((None, 1, s, vd), lambda bi, h, qi: (2 * k_off + h, bi, 0, 0)),
        ],
        out_specs=pl.BlockSpec((1, t, vd), lambda bi, h, qi: (bi, qi, h)),
        scratch_shapes=[pltpu.VMEM((vd, 2 * t), _BF16),
                        pltpu.VMEM((s // t, vd + ONES_ROWS, t), _BF16),
                        pltpu.VMEM((vd + ONES_ROWS, 2 * t), _F32),
                        pltpu.VMEM((1, 2 * t), _F32),
                        pltpu.VMEM((t, 2 * t), _F32),
                        pltpu.VMEM((t, 2 * t), _F32),
                        pltpu.VMEM((1, 2 * t), _F32),
                        pltpu.VMEM((1, 2 * t), _F32)],
        compiler_params=_params("parallel", "parallel", "arbitrary"),
        name="diff_attn",
    )(lam_qk, subln.reshape(1, vd), qkv, qkv, qkv)


HIST = 8
LRU_TILE = 128


def _lru_kernel(x_ref, wx_ref, wg_ref, cw_ref, cb_ref, wbd_ref, gb_ref, lam_ref, o_ref,
                xbuf, a_tm, u_tm, g_buf, h_ref):
    nb, t, d = x_ref.shape
    w = RNN_WIDTH
    lane_blocks = w // LANES

    @pl.when(pl.program_id(0) == 0)
    def _init():
        xbuf[:, 0:HIST, :] = jnp.zeros((nb, HIST, w), _F32)
        h_ref[...] = jnp.zeros_like(h_ref)

    x_all = x_ref[...].reshape(nb * t, d)
    xr_all = jnp.dot(x_all, wx_ref[...], preferred_element_type=_F32)
    g_buf[...] = jax.nn.gelu(jnp.dot(x_all, wg_ref[...], preferred_element_type=_F32))
    neg = -lam_ref[...]
    softplus = jnp.maximum(neg, 0.0) + jnp.log1p(jnp.exp(-jnp.abs(neg)))

    for b in range(nb):
        xbuf[b, HIST:HIST + t, :] = xr_all[b * t:(b + 1) * t]
        xc = cb_ref[...]
        for tap in range(CONV_WIDTH):
            start = HIST - (CONV_WIDTH - 1) + tap
            xc = xc + xbuf[b, start:start + t, :] * cw_ref[tap:tap + 1, :]
        xbuf[b, 0:HIST, :] = xbuf[b, t:t + HIST, :]

        xcb = xc.astype(_BF16)
        gates = []
        for g in range(2):
            parts = [jnp.dot(xcb[:, j * GATE_TILE:(j + 1) * GATE_TILE], wbd_ref[g, j],
                             preferred_element_type=_F32) for j in range(w // GATE_TILE)]
            gates.append(jnp.concatenate(parts, axis=1) + gb_ref[g:g + 1, :])
        r = jax.nn.sigmoid(gates[0])
        i = jax.nn.sigmoid(gates[1])
        log_a = -LRU_C * r * softplus
        a = jnp.exp(log_a)
        u = jnp.sqrt(-jnp.tanh(log_a) * (1.0 + a * a)) * (i * xc)
        rows = pl.ds(b, t, stride=nb)
        for lb in range(lane_blocks):
            a_tm[lb, rows, :] = a[:, lb * LANES:(lb + 1) * LANES]
            u_tm[lb, rows, :] = u[:, lb * LANES:(lb + 1) * LANES]

    def step(ts, h):
        rows = pl.ds(pl.multiple_of(ts * nb, nb), nb)
        out = []
        for lb in range(lane_blocks):
            h_lb = a_tm[lb, rows, :] * h[lb] + u_tm[lb, rows, :]
            u_tm[lb, rows, :] = h_lb
            out.append(h_lb)
        return tuple(out)

    h0 = tuple(h_ref[:, lb * LANES:(lb + 1) * LANES] for lb in range(lane_blocks))
    h_end = lax.fori_loop(0, t, step, h0, unroll=8)
    for lb in range(lane_blocks):
        h_ref[:, lb * LANES:(lb + 1) * LANES] = h_end[lb]

    for b in range(nb):
        rows = pl.ds(b, t, stride=nb)
        h_b = jnp.concatenate([u_tm[lb, rows, :] for lb in range(lane_blocks)], axis=1)
        o_ref[b] = (h_b * g_buf[b * t:(b + 1) * t, :]).astype(o_ref.dtype)


def _block_diag_gates(gate_w):
    per = GATE_TILE // LRU_BLOCK
    n_tiles = RNN_WIDTH // GATE_TILE
    gw = gate_w.reshape(2, n_tiles, per, LRU_BLOCK, LRU_BLOCK)
    eye = jnp.eye(per, dtype=gate_w.dtype)
    dense = jnp.einsum("gtpcd,pq->gtpcqd", gw, eye)
    return dense.reshape(2, n_tiles, GATE_TILE, GATE_TILE)


def _rg_lru(x16, w_rnn, conv_w, conv_b, gate_w, gate_b, lru_lambda):
    b, s, d = x16.shape
    t = min(LRU_TILE, s)
    assert s % t == 0
    w = RNN_WIDTH
    full = lambda shape: pl.BlockSpec(shape, lambda ti: (0,) * len(shape))
    return pl.pallas_call(
        _lru_kernel,
        out_shape=jax.ShapeDtypeStruct((b, s, w), _BF16),
        grid=(s // t,),
        in_specs=[pl.BlockSpec((b, t, d), lambda ti: (0, ti, 0)),
                  pl.BlockSpec((d, w), lambda ti: (0, 0)),
                  pl.BlockSpec((d, w), lambda ti: (0, 1)),
                  full((CONV_WIDTH, w)), full((1, w)),
                  full((2, w // GATE_TILE, GATE_TILE, GATE_TILE)), full((2, w)), full((1, w))],
        out_specs=pl.BlockSpec((b, t, w), lambda ti: (0, ti, 0)),
        scratch_shapes=[pltpu.VMEM((b, t + HIST, w), _F32),
                        pltpu.VMEM((w // LANES, t * b, LANES), _F32),
                        pltpu.VMEM((w // LANES, t * b, LANES), _F32),
                        pltpu.VMEM((t * b, w), _F32),
                        pltpu.VMEM((b, w), _F32)],
        compiler_params=_params("arbitrary"),
        name="rg_lru",
    )(x16, w_rnn, w_rnn, conv_w, conv_b.reshape(1, w), _block_diag_gates(gate_w).astype(_BF16), gate_b,
      lru_lambda.reshape(1, w))


DIL_TILE = DILATIONS[-1] * DIL_STEPS
DIL_UNROLL = 4


def _rows(start, size, stride):
    return pl.ds(start, size) if stride == 1 else pl.ds(start, size, stride=stride)


def _dil_kernel(q_ref, k_ref, v_ref, o_ref, qf, kf, vf, o_run, l_run):
    i = pl.program_id(1)
    t = DIL_TILE
    st = DIL_STEPS
    hd = HEAD_DIM

    pairs = DIL_HEADS // 2

    @pl.when(i == 0)
    def _no_history():
        kf[:, 0:t, :] = jnp.zeros((pairs, t, LANES), _F32)
        vf[:, 0:t, :] = jnp.zeros((pairs, t, LANES), _F32)

    @pl.when(i > 0)
    def _shift_history():
        kf[:, 0:t, :] = kf[:, t:2 * t, :]
        vf[:, 0:t, :] = vf[:, t:2 * t, :]

    for pair in range(pairs):
        lanes = slice(pair * LANES, (pair + 1) * LANES)
        qf[pair] = q_ref[0, :, lanes].astype(_F32) * (hd ** -0.5)
        kf[pair, t:2 * t, :] = k_ref[0, :, lanes].astype(_F32)
        vf[pair, t:2 * t, :] = v_ref[0, :, lanes].astype(_F32)

    key = lax.broadcasted_iota(jnp.int32, (2 * st, 2 * st), 0)
    qry = lax.broadcasted_iota(jnp.int32, (2 * st, 2 * st), 1) % st
    dist = st + qry - key
    band = (dist >= 0) & (dist <= st)
    own_block = key >= st
    first_head = _head0_lanes((LANES, st), 0)
    ones = jnp.ones((ONES_ROWS, 2 * st), _F32)

    def block(idx, dil, merge):
        phase = idx % dil
        n = idx // dil
        start = phase + dil * st * n
        q_rows = _rows(start, st, dil)
        kv_rows = _rows(t + start - dil * st, 2 * st, dil)
        valid = band & (own_block | (i > 0) | (n > 0))
        scores = []
        for pair in range(pairs):
            q_t = qf[pair, q_rows, :].T
            k_b = kf[pair, kv_rows, :].astype(_BF16)
            w = jnp.concatenate([jnp.where(first_head, q_t, 0.0),
                                 jnp.where(first_head, 0.0, q_t)], axis=1).astype(_BF16)
            scores.append(jnp.dot(k_b, w, preferred_element_type=_F32))
        o_parts, lse_rows = [], []
        for pair in range(pairs):
            v_t = vf[pair, kv_rows, :].T
            s = jnp.where(valid, scores[pair], -jnp.inf)
            m = _column_max(s)
            p = jnp.exp(s - m).astype(_BF16)
            lhs = jnp.concatenate([v_t[0:hd], ones, v_t[hd:2 * hd], ones], axis=0).astype(_BF16)
            pv = jnp.dot(lhs, p, preferred_element_type=_F32)
            for hh in range(2):
                rr = (hd + ONES_ROWS) * hh
                cc = slice(st * hh, st * (hh + 1))
                l = pv[rr + hd:rr + hd + 1, cc]
                o_parts.append(pv[rr:rr + hd, cc] / l)
                lse_rows.append(m[:, cc] + jnp.log(l))
        lse = jnp.concatenate(lse_rows, axis=0)
        if merge:
            run_l = l_run[q_rows, :].T[0:DIL_HEADS]
            mx = jnp.maximum(run_l, lse)
            w_run = jnp.exp(run_l - mx)
            w_cur = jnp.exp(lse - mx)
            den = w_run + w_cur
            a = w_run / den
            b = w_cur / den
            lse = mx + jnp.log(den)
        outs = []
        for pair in range(pairs):
            halves = o_parts[2 * pair:2 * pair + 2]
            if merge:
                run_o_t = o_run[pair, q_rows, :].T
                halves = [a[h:h + 1] * run_o_t[hh * hd:(hh + 1) * hd] + b[h:h + 1] * halves[hh]
                          for hh, h in enumerate((2 * pair, 2 * pair + 1))]
            outs.append(jnp.concatenate(halves, axis=0).T)
        pad = jnp.zeros((LANES - DIL_HEADS, st), _F32)
        return q_rows, outs, jnp.concatenate([lse, pad], axis=0).T

    def store(q_rows, outs, lse):
        for pair in range(pairs):
            o_run[pair, q_rows, :] = outs[pair]
        l_run[q_rows, :] = lse

    for idx_p, dil in enumerate(reversed(DILATIONS)):
        def body(trip, carry, dil=dil, merge=idx_p > 0):
            results = [block(DIL_UNROLL * trip + u, dil, merge) for u in range(DIL_UNROLL)]
            for res in results:
                store(*res)
            return carry
        lax.fori_loop(0, t // st // DIL_UNROLL, body, 0)

    for pair in range(pairs):
        o_ref[0, :, pair * LANES:(pair + 1) * LANES] = o_run[pair].astype(o_ref.dtype)


def _dilated_attention(qkv):
    b, s, _ = qkv.shape
    t = DIL_TILE
    assert s % t == 0
    return pl.pallas_call(
        _dil_kernel,
        out_shape=jax.ShapeDtypeStruct((b, s, DIL_WIDTH), _BF16),
        grid=(b, s // t),
        in_specs=[pl.BlockSpec((1, t, DIL_WIDTH), lambda bi, i: (bi, i, 0)),
                  pl.BlockSpec((1, t, DIL_WIDTH), lambda bi, i: (bi, i, 1)),
                  pl.BlockSpec((1, t, DIL_WIDTH), lambda bi, i: (bi, i, 2))],
        out_specs=pl.BlockSpec((1, t, DIL_WIDTH), lambda bi, i: (bi, i, 0)),
        scratch_shapes=[pltpu.VMEM((DIL_WIDTH // LANES, t, LANES), _F32),
                        pltpu.VMEM((DIL_WIDTH // LANES, 2 * t, LANES), _F32),
                        pltpu.VMEM((DIL_WIDTH // LANES, 2 * t, LANES), _F32),
                        pltpu.VMEM((DIL_WIDTH // LANES, t, LANES), _F32),
                        pltpu.VMEM((t, LANES), _F32)],
        compiler_params=_params("parallel", "arbitrary"),
        name="dil_attn",
    )(qkv, qkv, qkv)


def _merge_kernel(oa_ref, ob_ref, oc_ref, x16_ref, x_ref, wg_ref, wa_ref, wb_ref, wc_ref, wo_ref,
                  g_ref, b_ref, y32_ref, y16_ref):
    merged = None
    x16 = x16_ref[...]
    for idx, (o_ref, w_ref) in enumerate(((oa_ref, wa_ref), (ob_ref, wb_ref), (oc_ref, wc_ref))):
        gate = jnp.dot(x16, wg_ref[idx], preferred_element_type=_F32)
        branch = jnp.dot(o_ref[...], w_ref[...], preferred_element_type=_F32)
        term = jax.nn.sigmoid(gate) * branch
        merged = term if merged is None else merged + term
    h = jnp.dot(merged.astype(_BF16), wo_ref[...], preferred_element_type=_F32)
    y = _layer_norm(DEEPNORM_ALPHA * x_ref[...] + h, g_ref[...], b_ref[...])
    y32_ref[...] = y
    y16_ref[...] = y.astype(_BF16)


def _merge(o_a, o_b, o_c, x16, x, w_gates, w_a, w_b, w_c, w_o, g, b, tm=256):
    n, d = x.shape
    tm = min(tm, n)
    assert n % tm == 0
    row = lambda width: pl.BlockSpec((tm, width), lambda i: (i, 0))
    full = lambda shape: pl.BlockSpec(shape, lambda i: (0,) * len(shape))
    return pl.pallas_call(
        _merge_kernel,
        out_shape=(jax.ShapeDtypeStruct((n, d), _F32), jax.ShapeDtypeStruct((n, d), _BF16)),
        grid=(n // tm,),
        in_specs=[row(DIFF_WIDTH), row(RNN_WIDTH), row(DIL_WIDTH), row(d), row(d),
                  full((3, d, d)), full((DIFF_WIDTH, d)), full((RNN_WIDTH, d)), full((DIL_WIDTH, d)),
                  full((d, d)), full((1, d)), full((1, d))],
        out_specs=(row(d), row(d)),
        compiler_params=_params("parallel"),
        name="merge_ln1",
    )(o_a, o_b, o_c, x16, x, w_gates, w_a, w_b, w_c, w_o, g.reshape(1, d), b.reshape(1, d))


def _xattn_kernel(x32_ref, x16_ref, kv_ref, wq_ref, wo_ref, g_ref, b_ref, y32_ref, y16_ref):
    d = D_MODEL
    hd = XATTN_HEAD_DIM
    q = jnp.dot(x16_ref[0], wq_ref[...], preferred_element_type=_F32)
    q = (q * (hd ** -0.5)).astype(_BF16)
    scores = [lax.dot_general(q[:, h * hd:(h + 1) * hd], kv_ref[0, :, h * hd:(h + 1) * hd], _NT,
                              preferred_element_type=_F32) for h in range(XATTN_HEADS)]
    heads = []
    for h, s in enumerate(scores):
        v_h = kv_ref[0, :, d + h * hd:d + (h + 1) * hd]
        p = jnp.exp(s - jnp.max(s, axis=1, keepdims=True))
        l = jnp.sum(p, axis=1, keepdims=True)
        heads.append(jnp.dot(p.astype(_BF16), v_h, preferred_element_type=_F32) / l)
    o = jnp.concatenate(heads, axis=1).astype(_BF16)
    h_out = jnp.dot(o, wo_ref[...], preferred_element_type=_F32)
    y = _layer_norm(DEEPNORM_ALPHA * x32_ref[0] + h_out, g_ref[...], b_ref[...])
    y32_ref[0] = y
    y16_ref[0] = y.astype(_BF16)


def _cross_attention(x32, x16, kv, wq, wo, g, b, tm=512):
    bsz, s, d = x32.shape
    tm = min(tm, s)
    assert s % tm == 0
    row = pl.BlockSpec((1, tm, d), lambda bi, i: (bi, i, 0))
    full = lambda shape: pl.BlockSpec(shape, lambda bi, i: (0, 0))
    return pl.pallas_call(
        _xattn_kernel,
        out_shape=(jax.ShapeDtypeStruct((bsz, s, d), _F32), jax.ShapeDtypeStruct((bsz, s, d), _BF16)),
        grid=(bsz, s // tm),
        in_specs=[row, row, pl.BlockSpec((1, MEM_TOKENS, 2 * d), lambda bi, i: (bi, 0, 0)),
                  full((d, d)), full((d, d)), full((1, d)), full((1, d))],
        out_specs=(row, row),
        compiler_params=_params("parallel", "parallel"),
        name="xattn_ln2",
    )(x32, x16, kv, wq, wo, g.reshape(1, d), b.reshape(1, d))


def _ffn_kernel(x32_ref, x16_ref, wg_ref, wu_ref, wd_ref, g_ref, b_ref, y32_ref, y16_ref, acc_ref):
    c = pl.program_id(1)

    @pl.when(c == 0)
    def _init():
        acc_ref[...] = jnp.zeros_like(acc_ref)

    x = x16_ref[...]
    gate = jnp.dot(x, wg_ref[...], preferred_element_type=_F32)
    up = jnp.dot(x, wu_ref[...], preferred_element_type=_F32)
    hidden = (jax.nn.silu(gate) * up).astype(_BF16)
    acc_ref[...] += jnp.dot(hidden, wd_ref[...], preferred_element_type=_F32)

    @pl.when(c == pl.num_programs(1) - 1)
    def _fin():
        y = _layer_norm(DEEPNORM_ALPHA * x32_ref[...] + acc_ref[...], g_ref[...], b_ref[...])
        y32_ref[...] = y
        y16_ref[...] = y.astype(_BF16)


def _ffn(x32, x16, w_up, w_down, g, b, tm=512, chunks=2):
    n, d = x32.shape
    tm = min(tm, n)
    th = FFN_HIDDEN // chunks
    assert n % tm == 0 and FFN_HIDDEN % chunks == 0 and th % LANES == 0
    row = pl.BlockSpec((tm, d), lambda i, c: (i, 0))
    full = pl.BlockSpec((1, d), lambda i, c: (0, 0))
    return pl.pallas_call(
        _ffn_kernel,
        out_shape=(jax.ShapeDtypeStruct((n, d), _F32), jax.ShapeDtypeStruct((n, d), _BF16)),
        grid=(n // tm, chunks),
        in_specs=[row, row,
                  pl.BlockSpec((d, th), lambda i, c: (0, c)),
                  pl.BlockSpec((d, th), lambda i, c: (0, chunks + c)),
                  pl.BlockSpec((th, d), lambda i, c: (c, 0)),
                  full, full],
        out_specs=(row, row),
        scratch_shapes=[pltpu.VMEM((tm, d), _F32)],
        compiler_params=_params("parallel", "arbitrary"),
        name="ffn_ln3",
    )(x32, x16, w_up, w_up, w_down, g.reshape(1, d), b.reshape(1, d))


def kernel(x, mem, w_in, lam_qk, diff_subln, conv_w, conv_b, gate_w, gate_b, lru_lambda, w_br_a, w_br_b, w_br_c, w_out, ln1_g, ln1_b, xq, xkv, xo, ln2_g, ln2_b, w_up, w_down, ln3_g, ln3_b):
    bsz, seq, d = x.shape
    n = bsz * seq
    tables = _rotary_tables(seq)
    mem16 = mem.reshape(bsz * mem.shape[1], d).astype(_BF16)
    x32 = x.reshape(n, d)
    x16 = x32.astype(_BF16)
    for l in range(DEPTH):
        w = w_in[l].astype(_BF16)
        lambda_init = 0.8 - 0.6 * math.exp(-0.3 * l)
        w_qkv_a, w_qkv_c = (
            jnp.concatenate([_permute_qk_columns(w[:, c0:c0 + 1024]), w[:, c0 + 1024:c0 + 1536]], axis=1)
            for c0 in (COL_QK_A, COL_QK_C))
        qkv_a = _project("proj_qkv_a", x16, w_qkv_a, _BF16, seq, tables, rotary_width=1024, slabs=True)
        qkv_c = _project("proj_qkv_c", x16, w_qkv_c, _BF16, seq, tables, rotary_width=1024)
        w_gates = w[:, COL_GATE:COL_GATE + 3 * d].reshape(d, 3, d).transpose(1, 0, 2)

        o_a = _diff_attention(qkv_a.reshape(-1, bsz, seq, LANES), lam_qk[l], diff_subln[l], lambda_init)
        o_b = _rg_lru(x16.reshape(bsz, seq, d), w[:, COL_RNN:COL_RNN + 2 * RNN_WIDTH], conv_w[l], conv_b[l],
                      gate_w[l], gate_b[l], lru_lambda[l])
        o_c = _dilated_attention(qkv_c.reshape(bsz, seq, 1536))

        x32, x16 = _merge(o_a.reshape(n, 512), o_b.reshape(n, 1024), o_c.reshape(n, 512), x16, x32,
                          w_gates, w_br_a[l].astype(_BF16), w_br_b[l].astype(_BF16), w_br_c[l].astype(_BF16),
                          w_out[l].astype(_BF16), ln1_g[l], ln1_b[l])

        kv = _project("proj_mem_kv", mem16, xkv[l].astype(_BF16), _BF16, mem.shape[1])
        x32, x16 = _cross_attention(x32.reshape(bsz, seq, d), x16.reshape(bsz, seq, d),
                                    kv.reshape(bsz, mem.shape[1], 2 * d), xq[l].astype(_BF16),
                                    xo[l].astype(_BF16), ln2_g[l], ln2_b[l])
        x32, x16 = _ffn(x32.reshape(n, d), x16.reshape(n, d), w_up[l].astype(_BF16),
                        w_down[l].astype(_BF16), ln3_g[l], ln3_b[l])
    return x32.reshape(bsz, seq, d)
```

```python
import functools
import math

import jax
import jax.numpy as jnp
from jax import lax
from jax.experimental import pallas as pl
from jax.experimental.pallas import tpu as pltpu

D_MODEL = 1024
DEPTH = 2
HEAD_DIM = 64
ROT_DIM = HEAD_DIM // 4
ROPE_THETA = 500000.0
DIFF_HEADS = 4
DIFF_WIDTH = 512
DIL_HEADS = 8
DIL_WIDTH = 512
DIL_STEPS = 128
DILATIONS = (1, 4, 16)
RNN_WIDTH = D_MODEL
LRU_BLOCK = 64
LRU_C = 8.0
CONV_WIDTH = 4
MEM_TOKENS = 256
XATTN_HEADS = 4
XATTN_HEAD_DIM = D_MODEL // XATTN_HEADS
FFN_HIDDEN = 2816
DEEPNORM_ALPHA = (2 * DEPTH) ** 0.25
LN_EPS = 1e-5

COL_QK_A = 0
COL_RNN = 1536
COL_QK_C = 3584
COL_GATE = 5120

LANES = 128
GATE_TILE = 256
V7X_VMEM_BYTES = 64 * 1024 * 1024
VMEM_LIMIT = V7X_VMEM_BYTES * 3 // 4

_BF16 = jnp.bfloat16
_F32 = jnp.float32
_NT = (((1,), (1,)), ((), ()))


def _params(*semantics):
    return pltpu.CompilerParams(dimension_semantics=semantics, vmem_limit_bytes=VMEM_LIMIT)


def _layer_norm(z, g, b):
    mu = jnp.mean(z, axis=-1, keepdims=True)
    d = z - mu
    var = jnp.mean(d * d, axis=-1, keepdims=True)
    return d * lax.rsqrt(var + LN_EPS) * g + b


def _cast_kernel(x_ref, o_ref):
    o_ref[...] = x_ref[...].astype(o_ref.dtype)


def _to_bf16(x2, tm=1024):
    n, d = x2.shape
    tm = min(tm, n)
    assert n % tm == 0
    spec = pl.BlockSpec((tm, d), lambda i: (i, 0))
    return pl.pallas_call(_cast_kernel, out_shape=jax.ShapeDtypeStruct((n, d), _BF16), grid=(n // tm,),
                          in_specs=[spec], out_specs=spec, compiler_params=_params("parallel"),
                          name="cast_bf16")(x2)


MXU_WIDTH = 256


def _proj_kernel(x_ref, w_ref, *rest, rotary_width):
    o_ref = rest[-1]
    slabs = len(o_ref.shape) == 3
    x = x_ref[...]
    width = w_ref.shape[1]
    accs = [jnp.dot(x, w_ref[:, c0:c0 + MXU_WIDTH], preferred_element_type=_F32)
            for c0 in range(0, width, MXU_WIDTH)]
    if rotary_width:
        c, s = rest[0][...], rest[1][...]
    per = MXU_WIDTH // LANES
    for g, acc in enumerate(accs):
        for jj in range(per):
            j = g * per + jj
            blk = acc[:, jj * LANES:(jj + 1) * LANES]
            if j * LANES < rotary_width:
                partner = pltpu.roll(blk, LANES // 2, axis=1)
                blk = blk * c + partner * s
            if slabs:
                o_ref[j] = blk.astype(o_ref.dtype)
            else:
                o_ref[:, j * LANES:(j + 1) * LANES] = blk.astype(o_ref.dtype)


_HALF = ROT_DIM // 2
QK_LANE_ORDER = (list(range(0, _HALF)) + list(range(HEAD_DIM, HEAD_DIM + _HALF))
                 + list(range(ROT_DIM, HEAD_DIM))
                 + list(range(_HALF, ROT_DIM)) + list(range(HEAD_DIM + _HALF, HEAD_DIM + ROT_DIM))
                 + list(range(HEAD_DIM + ROT_DIM, 2 * HEAD_DIM)))


def _head0_lanes(shape, axis):
    lane = lax.broadcasted_iota(jnp.int32, shape, axis)
    return (lane < _HALF) | ((lane >= ROT_DIM) & (lane < HEAD_DIM + _HALF))


def _permute_qk_columns(w_qk):
    d, width = w_qk.shape
    grouped = w_qk.reshape(d, width // LANES, LANES)
    return jnp.take(grouped, jnp.asarray(QK_LANE_ORDER, jnp.int32), axis=2).reshape(d, width)


def _rotary_tables(seq):
    inv_freq = jnp.power(ROPE_THETA, -2.0 * jnp.arange(_HALF, dtype=_F32) / ROT_DIM)
    ang = jnp.arange(seq).astype(_F32)[:, None] * inv_freq[None, :]
    cos, sin = jnp.cos(ang), jnp.sin(ang)
    ones = jnp.ones((seq, HEAD_DIM - ROT_DIM), _F32)
    zeros = jnp.zeros((seq, HEAD_DIM - ROT_DIM), _F32)
    c = jnp.concatenate([cos, cos, ones, cos, cos, ones], axis=1)
    s = jnp.concatenate([-sin, -sin, zeros, sin, sin, zeros], axis=1)
    return c, s


def _project(name, x2, w, out_dtype, seq, tables=None, rotary_width=0, slabs=False, tm=1024):
    n, k = x2.shape
    width = w.shape[1]
    tm = min(tm, seq)
    assert n % tm == 0 and seq % tm == 0 and width % MXU_WIDTH == 0 and rotary_width % LANES == 0
    assert (tables is None) == (rotary_width == 0)
    in_specs = [pl.BlockSpec((tm, k), lambda i: (i, 0)), pl.BlockSpec((k, width), lambda i: (0, 0))]
    args = [x2, w]
    if tables is not None:
        in_specs += [pl.BlockSpec((tm, LANES), lambda i: (i % (seq // tm), 0))] * len(tables)
        args += list(tables)
    if slabs:
        out_shape = jax.ShapeDtypeStruct((width // LANES, n, LANES), out_dtype)
        out_spec = pl.BlockSpec((width // LANES, tm, LANES), lambda i: (0, i, 0))
    else:
        out_shape = jax.ShapeDtypeStruct((n, width), out_dtype)
        out_spec = pl.BlockSpec((tm, width), lambda i: (i, 0))
    return pl.pallas_call(
        functools.partial(_proj_kernel, rotary_width=rotary_width),
        out_shape=out_shape,
        grid=(n // tm,),
        in_specs=in_specs,
        out_specs=out_spec,
        compiler_params=_params("parallel"),
        name=name,
    )(*args)


ONES_ROWS = 16
ATTN_SUB = 256


def _column_max(x):
    rows = x.shape[0]
    while rows > 8 and rows % 2 == 0:
        rows //= 2
        x = jnp.maximum(x[:rows], x[rows:])
    return jnp.max(x, axis=0, keepdims=True)


def _diff_attn_kernel(lam_ref, g_ref, q_ref, k_ref, v_ref, o_ref, wq_scr, vt_scr, acc_scr, m_scr,
                      sa_scr, sb_scr, ma_scr, mb_scr, *, lambda_init):
    qi = pl.program_id(2)
    t = q_ref.shape[1]
    n_kv = k_ref.shape[1] // t
    vd = 2 * HEAD_DIM

    @pl.when(qi == 0)
    def _transpose_v():
        for j in range(n_kv):
            vt_scr[j, 0:vd, :] = v_ref[0, j * t:(j + 1) * t, :].astype(_F32).T.astype(_BF16)
            vt_scr[j, vd:vd + ONES_ROWS, :] = jnp.ones((ONES_ROWS, t), _BF16)

    q_t = (q_ref[0].astype(_F32) * (HEAD_DIM ** -0.5)).T
    first = _head0_lanes(q_t.shape, 0)
    wq_scr[:, 0:t] = jnp.where(first, q_t, 0.0).astype(_BF16)
    wq_scr[:, t:2 * t] = jnp.where(first, 0.0, q_t).astype(_BF16)
    acc_scr[...] = jnp.zeros_like(acc_scr)
    m_scr[...] = jnp.full_like(m_scr, -jnp.inf)

    chains = [slice(c * ATTN_SUB, (c + 1) * ATTN_SUB) for c in range(2 * t // ATTN_SUB)]

    def scores(j, buf, cols):
        s_ref, max_ref = buf
        k_t = k_ref[0, pl.ds(pl.multiple_of(j * t, t), t), :]
        s = jnp.dot(k_t, wq_scr[:, cols], preferred_element_type=_F32)
        s_ref[:, cols] = s
        max_ref[:, cols] = _column_max(s)

    def softmax_pv(j, buf, cols, masked):
        s_ref, max_ref = buf
        s = s_ref[:, cols]
        if masked:
            key = lax.broadcasted_iota(jnp.int32, (t, ATTN_SUB), 0)
            qry = lax.broadcasted_iota(jnp.int32, (t, ATTN_SUB), 1) + cols.start % t
            s = jnp.where(key <= qry, s, -jnp.inf)
            m_tile = _column_max(s)
        else:
            m_tile = max_ref[:, cols]
        m_prev = m_scr[:, cols]
        m_new = jnp.maximum(m_prev, m_tile)
        alpha = jnp.exp(m_prev - m_new)
        p = jnp.exp(s - m_new).astype(_BF16)
        acc_scr[:, cols] = alpha * acc_scr[:, cols] + jnp.dot(vt_scr[j], p, preferred_element_type=_F32)
        m_scr[:, cols] = m_new

    def stage(j_next, s_next, j, s_cur, masked):
        for cols in chains:
            if j_next is not None:
                scores(j_next, s_next, cols)
            softmax_pv(j, s_cur, cols, masked)

    buf_a = (sa_scr, ma_scr)
    buf_b = (sb_scr, mb_scr)
    for cols in chains:
        scores(0, buf_a, cols)

    def body(jj, carry):
        j = 2 * jj
        stage(j + 1, buf_b, j, buf_a, False)
        stage(j + 2, buf_a, j + 1, buf_b, False)
        return carry

    lax.fori_loop(0, qi // 2, body, 0)

    @pl.when(qi % 2 == 0)
    def _even_tail():
        stage(None, None, qi, buf_a, True)

    @pl.when(qi % 2 == 1)
    def _odd_tail():
        stage(qi, buf_b, qi - 1, buf_a, False)
        stage(None, None, qi, buf_b, True)

    lq = lam_ref[...]
    lam = (jnp.exp(jnp.sum(lq[0:1] * lq[1:2], axis=1, keepdims=True))
           - jnp.exp(jnp.sum(lq[2:3] * lq[3:4], axis=1, keepdims=True)) + lambda_init)
    o1 = acc_scr[0:vd, 0:t] / acc_scr[vd:vd + 1, 0:t]
    o2 = acc_scr[0:vd, t:2 * t] / acc_scr[vd:vd + 1, t:2 * t]
    o = (o1 - lam * o2).T
    y = o * lax.rsqrt(jnp.mean(o * o, axis=1, keepdims=True) + LN_EPS) * g_ref[...]
    o_ref[0] = (y * (1.0 - lambda_init)).astype(o_ref.dtype)


def _diff_attention(qkv, lam_qk, subln, lambda_init, tile=512):
    _, b, s, vd = qkv.shape
    t = min(tile, s)
    assert s % t == 0 and (2 * t) % ATTN_SUB == 0 and t % ATTN_SUB == 0 and vd == 2 * HEAD_DIM
    k_off = DIFF_WIDTH // vd
    return pl.pallas_call(
        functools.partial(_diff_attn_kernel, lambda_init=lambda_init),
        out_shape=jax.ShapeDtypeStruct((b, s, DIFF_WIDTH), _BF16),
        grid=(b, DIFF_HEADS, s // t),
        in_specs=[
            pl.BlockSpec((4, HEAD_DIM), lambda bi, h, qi: (0, 0)),
            pl.BlockSpec((1, vd), lambda bi, h, qi: (0, 0)),
            pl.BlockSpec((None, 1, t, vd), lambda bi, h, qi: (h, bi, qi, 0)),
            pl.BlockSpec((None, 1, s, vd), lambda bi, h, qi: (k_off + h, bi, 0, 0)),
            pl.BlockSpec((None, 1, s, vd), lambda bi, h, qi: (2 * k_off + h, bi, 0, 0)),
        ],
        out_specs=pl.BlockSpec((1, t, vd), lambda bi, h, qi: (bi, qi, h)),
        scratch_shapes=[pltpu.VMEM((vd, 2 * t), _BF16),
                        pltpu.VMEM((s // t, vd + ONES_ROWS, t), _BF16),
                        pltpu.VMEM((vd + ONES_ROWS, 2 * t), _F32),
                        pltpu.VMEM((1, 2 * t), _F32),
                        pltpu.VMEM((t, 2 * t), _F32),
                        pltpu.VMEM((t, 2 * t), _F32),
                        pltpu.VMEM((1, 2 * t), _F32),
                        pltpu.VMEM((1, 2 * t), _F32)],
        compiler_params=_params("parallel", "parallel", "arbitrary"),
        name="diff_attn",
    )(lam_qk, subln.reshape(1, vd), qkv, qkv, qkv)


HIST = 8
LRU_TILE = 128


def _lru_kernel(x_ref, wx_ref, wg_ref, cw_ref, cb_ref, wbd_ref, gb_ref, lam_ref, o_ref,
                xbuf, a_tm, u_tm, g_buf, h_ref):
    nb, t, d = x_ref.shape
    w = RNN_WIDTH
    lane_blocks = w // LANES

    @pl.when(pl.program_id(0) == 0)
    def _init():
        xbuf[:, 0:HIST, :] = jnp.zeros((nb, HIST, w), _F32)
        h_ref[...] = jnp.zeros_like(h_ref)

    x_all = x_ref[...].reshape(nb * t, d)
    xr_all = jnp.dot(x_all, wx_ref[...], preferred_element_type=_F32)
    g_buf[...] = jax.nn.gelu(jnp.dot(x_all, wg_ref[...], preferred_element_type=_F32))
    neg = -lam_ref[...]
    softplus = jnp.maximum(neg, 0.0) + jnp.log1p(jnp.exp(-jnp.abs(neg)))

    for b in range(nb):
        xbuf[b, HIST:HIST + t, :] = xr_all[b * t:(b + 1) * t]
        xc = cb_ref[...]
        for tap in range(CONV_WIDTH):
            start = HIST - (CONV_WIDTH - 1) + tap
            xc = xc + xbuf[b, start:start + t, :] * cw_ref[tap:tap + 1, :]
        xbuf[b, 0:HIST, :] = xbuf[b, t:t + HIST, :]

        xcb = xc.astype(_BF16)
        gates = []
        for g in range(2):
            parts = [jnp.dot(xcb[:, j * GATE_TILE:(j + 1) * GATE_TILE], wbd_ref[g, j],
                             preferred_element_type=_F32) for j in range(w // GATE_TILE)]
            gates.append(jnp.concatenate(parts, axis=1) + gb_ref[g:g + 1, :])
        r = jax.nn.sigmoid(gates[0])
        i = jax.nn.sigmoid(gates[1])
        log_a = -LRU_C * r * softplus
        a = jnp.exp(log_a)
        u = jnp.sqrt(-jnp.tanh(log_a) * (1.0 + a * a)) * (i * xc)
        rows = pl.ds(b, t, stride=nb)
        for lb in range(lane_blocks):
            a_tm[lb, rows, :] = a[:, lb * LANES:(lb + 1) * LANES]
            u_tm[lb, rows, :] = u[:, lb * LANES:(lb + 1) * LANES]

    def step(ts, h):
        rows = pl.ds(pl.multiple_of(ts * nb, nb), nb)
        out = []
        for lb in range(lane_blocks):
            h_lb = a_tm[lb, rows, :] * h[lb] + u_tm[lb, rows, :]
            u_tm[lb, rows, :] = h_lb
            out.append(h_lb)
        return tuple(out)

    h0 = tuple(h_ref[:, lb * LANES:(lb + 1) * LANES] for lb in range(lane_blocks))
    h_end = lax.fori_loop(0, t, step, h0, unroll=8)
    for lb in range(lane_blocks):
        h_ref[:, lb * LANES:(lb + 1) * LANES] = h_end[lb]

    for b in range(nb):
        rows = pl.ds(b, t, stride=nb)
        h_b = jnp.concatenate([u_tm[lb, rows, :] for lb in range(lane_blocks)], axis=1)
        o_ref[b] = (h_b * g_buf[b * t:(b + 1) * t, :]).astype(o_ref.dtype)


def _block_diag_gates(gate_w):
    per = GATE_TILE // LRU_BLOCK
    n_tiles = RNN_WIDTH // GATE_TILE
    gw = gate_w.reshape(2, n_tiles, per, LRU_BLOCK, LRU_BLOCK)
    eye = jnp.eye(per, dtype=gate_w.dtype)
    dense = jnp.einsum("gtpcd,pq->gtpcqd", gw, eye)
    return dense.reshape(2, n_tiles, GATE_TILE, GATE_TILE)


def _rg_lru(x16, w_rnn, conv_w, conv_b, gate_w, gate_b, lru_lambda):
    b, s, d = x16.shape
    t = min(LRU_TILE, s)
    assert s % t == 0
    w = RNN_WIDTH
    full = lambda shape: pl.BlockSpec(shape, lambda ti: (0,) * len(shape))
    return pl.pallas_call(
        _lru_kernel,
        out_shape=jax.ShapeDtypeStruct((b, s, w), _BF16),
        grid=(s // t,),
        in_specs=[pl.BlockSpec((b, t, d), lambda ti: (0, ti, 0)),
                  pl.BlockSpec((d, w), lambda ti: (0, 0)),
                  pl.BlockSpec((d, w), lambda ti: (0, 1)),
                  full((CONV_WIDTH, w)), full((1, w)),
                  full((2, w // GATE_TILE, GATE_TILE, GATE_TILE)), full((2, w)), full((1, w))],
        out_specs=pl.BlockSpec((b, t, w), lambda ti: (0, ti, 0)),
        scratch_shapes=[pltpu.VMEM((b, t + HIST, w), _F32),
                        pltpu.VMEM((w // LANES, t * b, LANES), _F32),
                        pltpu.VMEM((w // LANES, t * b, LANES), _F32),
                        pltpu.VMEM((t * b, w), _F32),
                        pltpu.VMEM((b, w), _F32)],
        compiler_params=_params("arbitrary"),
        name="rg_lru",
    )(x16, w_rnn, w_rnn, conv_w, conv_b.reshape(1, w), _block_diag_gates(gate_w).astype(_BF16), gate_b,
      lru_lambda.reshape(1, w))


DIL_TILE = DILATIONS[-1] * DIL_STEPS
DIL_UNROLL = 4


def _rows(start, size, stride):
    return pl.ds(start, size) if stride == 1 else pl.ds(start, size, stride=stride)


def _dil_kernel(q_ref, k_ref, v_ref, o_ref, qf, kf, vf, o_run, l_run):
    i = pl.program_id(1)
    t = DIL_TILE
    st = DIL_STEPS
    hd = HEAD_DIM

    pairs = DIL_HEADS // 2

    @pl.when(i == 0)
    def _no_history():
        kf[:, 0:t, :] = jnp.zeros((pairs, t, LANES), _F32)
        vf[:, 0:t, :] = jnp.zeros((pairs, t, LANES), _F32)

    @pl.when(i > 0)
    def _shift_history():
        kf[:, 0:t, :] = kf[:, t:2 * t, :]
        vf[:, 0:t, :] = vf[:, t:2 * t, :]

    for pair in range(pairs):
        lanes = slice(pair * LANES, (pair + 1) * LANES)
        qf[pair] = q_ref[0, :, lanes].astype(_F32) * (hd ** -0.5)
        kf[pair, t:2 * t, :] = k_ref[0, :, lanes].astype(_F32)
        vf[pair, t:2 * t, :] = v_ref[0, :, lanes].astype(_F32)

    key = lax.broadcasted_iota(jnp.int32, (2 * st, 2 * st), 0)
    qry = lax.broadcasted_iota(jnp.int32, (2 * st, 2 * st), 1) % st
    dist = st + qry - key
    band = (dist >= 0) & (dist <= st)
    own_block = key >= st
    first_head = _head0_lanes((LANES, st), 0)
    ones = jnp.ones((ONES_ROWS, 2 * st), _F32)

    def block(idx, dil, merge):
        phase = idx % dil
        n = idx // dil
        start = phase + dil * st * n
        q_rows = _rows(start, st, dil)
        kv_rows = _rows(t + start - dil * st, 2 * st, dil)
        valid = band & (own_block | (i > 0) | (n > 0))
        scores = []
        for pair in range(pairs):
            q_t = qf[pair, q_rows, :].T
            k_b = kf[pair, kv_rows, :].astype(_BF16)
            w = jnp.concatenate([jnp.where(first_head, q_t, 0.0),
                                 jnp.where(first_head, 0.0, q_t)], axis=1).astype(_BF16)
            scores.append(jnp.dot(k_b, w, preferred_element_type=_F32))
        o_parts, lse_rows = [], []
        for pair in range(pairs):
            v_t = vf[pair, kv_rows, :].T
            s = jnp.where(valid, scores[pair], -jnp.inf)
            m = _column_max(s)
            p = jnp.exp(s - m).astype(_BF16)
            lhs = jnp.concatenate([v_t[0:hd], ones, v_t[hd:2 * hd], ones], axis=0).astype(_BF16)
            pv = jnp.dot(lhs, p, preferred_element_type=_F32)
            for hh in range(2):
                rr = (hd + ONES_ROWS) * hh
                cc = slice(st * hh, st * (hh + 1))
                l = pv[rr + hd:rr + hd + 1, cc]
                o_parts.append(pv[rr:rr + hd, cc] / l)
                lse_rows.append(m[:, cc] + jnp.log(l))
        lse = jnp.concatenate(lse_rows, axis=0)
        if merge:
            run_l = l_run[q_rows, :].T[0:DIL_HEADS]
            mx = jnp.maximum(run_l, lse)
            w_run = jnp.exp(run_l - mx)
            w_cur = jnp.exp(lse - mx)
            den = w_run + w_cur
            a = w_run / den
            b = w_cur / den
            lse = mx + jnp.log(den)
        outs = []
        for pair in range(pairs):
            halves = o_parts[2 * pair:2 * pair + 2]
            if merge:
                run_o_t = o_run[pair, q_rows, :].T
                halves = [a[h:h + 1] * run_o_t[hh * hd:(hh + 1) * hd] + b[h:h + 1] * halves[hh]
                          for hh, h in enumerate((2 * pair, 2 * pair + 1))]
            outs.append(jnp.concatenate(halves, axis=0).T)
        pad = jnp.zeros((LANES - DIL_HEADS, st), _F32)
        return q_rows, outs, jnp.concatenate([lse, pad], axis=0).T

    def store(q_rows, outs, lse):
        for pair in range(pairs):
            o_run[pair, q_rows, :] = outs[pair]
        l_run[q_rows, :] = lse

    for idx_p, dil in enumerate(reversed(DILATIONS)):
        def body(trip, carry, dil=dil, merge=idx_p > 0):
            results = [block(DIL_UNROLL * trip + u, dil, merge) for u in range(DIL_UNROLL)]
            for res in results:
                store(*res)
            return carry
        lax.fori_loop(0, t // st // DIL_UNROLL, body, 0)

    for pair in range(pairs):
        o_ref[0, :, pair * LANES:(pair + 1) * LANES] = o_run[pair].astype(o_ref.dtype)


def _dilated_attention(qkv):
    b, s, _ = qkv.shape
    t = DIL_TILE
    assert s % t == 0
    return pl.pallas_call(
        _dil_kernel,
        out_shape=jax.ShapeDtypeStruct((b, s, DIL_WIDTH), _BF16),
        grid=(b, s // t),
        in_specs=[pl.BlockSpec((1, t, DIL_WIDTH), lambda bi, i: (bi, i, 0)),
                  pl.BlockSpec((1, t, DIL_WIDTH), lambda bi, i: (bi, i, 1)),
                  pl.BlockSpec((1, t, DIL_WIDTH), lambda bi, i: (bi, i, 2))],
        out_specs=pl.BlockSpec((1, t, DIL_WIDTH), lambda bi, i: (bi, i, 0)),
        scratch_shapes=[pltpu.VMEM((DIL_WIDTH // LANES, t, LANES), _F32),
                        pltpu.VMEM((DIL_WIDTH // LANES, 2 * t, LANES), _F32),
                        pltpu.VMEM((DIL_WIDTH // LANES, 2 * t, LANES), _F32),
                        pltpu.VMEM((DIL_WIDTH // LANES, t, LANES), _F32),
                        pltpu.VMEM((t, LANES), _F32)],
        compiler_params=_params("parallel", "arbitrary"),
        name="dil_attn",
    )(qkv, qkv, qkv)


def _merge_kernel(oa_ref, ob_ref, oc_ref, x16_ref, x_ref, wg_ref, wa_ref, wb_ref, wc_ref, wo_ref,
                  g_ref, b_ref, y32_ref, y16_ref):
    tm = x_ref.shape[0]
    halves = [slice(0, tm // 2), slice(tm // 2, tm)] if tm % 32 == 0 else [slice(0, tm)]
    branches = ((oa_ref, wa_ref), (ob_ref, wb_ref), (oc_ref, wc_ref))
    merged = []
    for rows in halves:
        x16 = x16_ref[rows, :]
        acc = None
        for idx, (o_ref, w_ref) in enumerate(branches):
            gate = jnp.dot(x16, wg_ref[idx], preferred_element_type=_F32)
            branch = jnp.dot(o_ref[rows, :], w_ref[...], preferred_element_type=_F32)
            term = jax.nn.sigmoid(gate) * branch
            acc = term if acc is None else acc + term
        merged.append(acc.astype(_BF16))
    projected = [jnp.dot(m, wo_ref[...], preferred_element_type=_F32) for m in merged]
    for rows, h in zip(halves, projected):
        y = _layer_norm(DEEPNORM_ALPHA * x_ref[rows, :] + h, g_ref[...], b_ref[...])
        y32_ref[rows, :] = y
        y16_ref[rows, :] = y.astype(_BF16)


def _merge(o_a, o_b, o_c, x16, x, w_gates, w_a, w_b, w_c, w_o, g, b, tm=512):
    n, d = x.shape
    tm = min(tm, n)
    assert n % tm == 0
    row = lambda width: pl.BlockSpec((tm, width), lambda i: (i, 0))
    full = lambda shape: pl.BlockSpec(shape, lambda i: (0,) * len(shape), pipeline_mode=pl.Buffered(1))
    return pl.pallas_call(
        _merge_kernel,
        out_shape=(jax.ShapeDtypeStruct((n, d), _F32), jax.ShapeDtypeStruct((n, d), _BF16)),
        grid=(n // tm,),
        in_specs=[row(DIFF_WIDTH), row(RNN_WIDTH), row(DIL_WIDTH), row(d), row(d),
                  full((3, d, d)), full((DIFF_WIDTH, d)), full((RNN_WIDTH, d)), full((DIL_WIDTH, d)),
                  full((d, d)), full((1, d)), full((1, d))],
        out_specs=(row(d), row(d)),
        compiler_params=_params("parallel"),
        name="merge_ln1",
    )(o_a, o_b, o_c, x16, x, w_gates, w_a, w_b, w_c, w_o, g.reshape(1, d), b.reshape(1, d))


def _xattn_kernel(x32_ref, x16_ref, kv_ref, wq_ref, wo_ref, g_ref, b_ref, y32_ref, y16_ref):
    d = D_MODEL
    hd = XATTN_HEAD_DIM
    tm = x16_ref.shape[1]
    halves = [slice(0, tm // 2), slice(tm // 2, tm)] if tm % 16 == 0 else [slice(0, tm)]
    qs = []
    for rows in halves:
        q = jnp.dot(x16_ref[0, rows, :], wq_ref[...], preferred_element_type=_F32)
        qs.append((q * (hd ** -0.5)).astype(_BF16))
    scores = [[lax.dot_general(q[:, h * hd:(h + 1) * hd], kv_ref[0, :, h * hd:(h + 1) * hd], _NT,
                               preferred_element_type=_F32) for h in range(XATTN_HEADS)] for q in qs]
    attended = []
    for half_scores in scores:
        heads = []
        for h, s in enumerate(half_scores):
            v_h = kv_ref[0, :, d + h * hd:d + (h + 1) * hd]
            p = jnp.exp(s - jnp.max(s, axis=1, keepdims=True))
            l = jnp.sum(p, axis=1, keepdims=True)
            heads.append(jnp.dot(p.astype(_BF16), v_h, preferred_element_type=_F32) / l)
        attended.append(jnp.concatenate(heads, axis=1).astype(_BF16))
    projected = [jnp.dot(o, wo_ref[...], preferred_element_type=_F32) for o in attended]
    for rows, h_out in zip(halves, projected):
        y = _layer_norm(DEEPNORM_ALPHA * x32_ref[0, rows, :] + h_out, g_ref[...], b_ref[...])
        y32_ref[0, rows, :] = y
        y16_ref[0, rows, :] = y.astype(_BF16)


def _cross_attention(x32, x16, kv, wq, wo, g, b, tm=512):
    bsz, s, d = x32.shape
    tm = min(tm, s)
    assert s % tm == 0
    row = pl.BlockSpec((1, tm, d), lambda bi, i: (bi, i, 0))
    full = lambda shape: pl.BlockSpec(shape, lambda bi, i: (0, 0))
    return pl.pallas_call(
        _xattn_kernel,
        out_shape=(jax.ShapeDtypeStruct((bsz, s, d), _F32), jax.ShapeDtypeStruct((bsz, s, d), _BF16)),
        grid=(bsz, s // tm),
        in_specs=[row, row, pl.BlockSpec((1, MEM_TOKENS, 2 * d), lambda bi, i: (bi, 0, 0)),
                  full((d, d)), full((d, d)), full((1, d)), full((1, d))],
        out_specs=(row, row),
        compiler_params=_params("parallel", "parallel"),
        name="xattn_ln2",
    )(x32, x16, kv, wq, wo, g.reshape(1, d), b.reshape(1, d))


def _ffn_kernel(x32_ref, x16_ref, wg_ref, wu_ref, wd_ref, g_ref, b_ref, y32_ref, y16_ref, acc_ref):
    c = pl.program_id(1)

    @pl.when(c == 0)
    def _init():
        acc_ref[...] = jnp.zeros_like(acc_ref)

    x = x16_ref[...]
    gate = jnp.dot(x, wg_ref[...], preferred_element_type=_F32)
    up = jnp.dot(x, wu_ref[...], preferred_element_type=_F32)
    hidden = (jax.nn.silu(gate) * up).astype(_BF16)
    acc_ref[...] += jnp.dot(hidden, wd_ref[...], preferred_element_type=_F32)

    @pl.when(c == pl.num_programs(1) - 1)
    def _fin():
        y = _layer_norm(DEEPNORM_ALPHA * x32_ref[...] + acc_ref[...], g_ref[...], b_ref[...])
        y32_ref[...] = y
        y16_ref[...] = y.astype(_BF16)


def _ffn(x32, x16, w_up, w_down, g, b, tm=512, chunks=2):
    n, d = x32.shape
    tm = min(tm, n)
    th = FFN_HIDDEN // chunks
    assert n % tm == 0 and FFN_HIDDEN % chunks == 0 and th % LANES == 0
    row = pl.BlockSpec((tm, d), lambda i, c: (i, 0))
    full = pl.BlockSpec((1, d), lambda i, c: (0, 0))
    return pl.pallas_call(
        _ffn_kernel,
        out_shape=(jax.ShapeDtypeStruct((n, d), _F32), jax.ShapeDtypeStruct((n, d), _BF16)),
        grid=(n // tm, chunks),
        in_specs=[row, row,
                  pl.BlockSpec((d, th), lambda i, c: (0, c)),
                  pl.BlockSpec((d, th), lambda i, c: (0, chunks + c)),
                  pl.BlockSpec((th, d), lambda i, c: (c, 0)),
                  full, full],
        out_specs=(row, row),
        scratch_shapes=[pltpu.VMEM((tm, d), _F32)],
        compiler_params=_params("parallel", "arbitrary"),
        name="ffn_ln3",
    )(x32, x16, w_up, w_up, w_down, g.reshape(1, d), b.reshape(1, d))


def kernel(x, mem, w_in, lam_qk, diff_subln, conv_w, conv_b, gate_w, gate_b, lru_lambda, w_br_a, w_br_b, w_br_c, w_out, ln1_g, ln1_b, xq, xkv, xo, ln2_g, ln2_b, w_up, w_down, ln3_g, ln3_b):
    bsz, seq, d = x.shape
    n = bsz * seq
    tables = _rotary_tables(seq)
    mem16 = mem.reshape(bsz * mem.shape[1], d).astype(_BF16)
    x32 = x.reshape(n, d)
    x16 = _to_bf16(x32)
    for l in range(DEPTH):
        w = w_in[l].astype(_BF16)
        lambda_init = 0.8 - 0.6 * math.exp(-0.3 * l)
        w_qkv_a, w_qkv_c = (
            jnp.concatenate([_permute_qk_columns(w[:, c0:c0 + 1024]), w[:, c0 + 1024:c0 + 1536]], axis=1)
            for c0 in (COL_QK_A, COL_QK_C))
        qkv_a = _project("proj_qkv_a", x16, w_qkv_a, _BF16, seq, tables, rotary_width=1024, slabs=True)
        qkv_c = _project("proj_qkv_c", x16, w_qkv_c, _BF16, seq, tables, rotary_width=1024)
        w_gates = w[:, COL_GATE:COL_GATE + 3 * d].reshape(d, 3, d).transpose(1, 0, 2)

        o_a = _diff_attention(qkv_a.reshape(-1, bsz, seq, LANES), lam_qk[l], diff_subln[l], lambda_init)
        o_b = _rg_lru(x16.reshape(bsz, seq, d), w[:, COL_RNN:COL_RNN + 2 * RNN_WIDTH], conv_w[l], conv_b[l],
                      gate_w[l], gate_b[l], lru_lambda[l])
        o_c = _dilated_attention(qkv_c.reshape(bsz, seq, 1536))

        x32, x16 = _merge(o_a.reshape(n, 512), o_b.reshape(n, 1024), o_c.reshape(n, 512), x16, x32,
                          w_gates, w_br_a[l].astype(_BF16), w_br_b[l].astype(_BF16), w_br_c[l].astype(_BF16),
                          w_out[l].astype(_BF16), ln1_g[l], ln1_b[l])

        kv = _project("proj_mem_kv", mem16, xkv[l].astype(_BF16), _BF16, mem.shape[1])
        x32, x16 = _cross_attention(x32.reshape(bsz, seq, d), x16.reshape(bsz, seq, d),
                                    kv.reshape(bsz, mem.shape[1], 2 * d), xq[l].astype(_BF16),
                                    xo[l].astype(_BF16), ln2_g[l], ln2_b[l])
        x32, x16 = _ffn(x32.reshape(n, d), x16.reshape(n, d), w_up[l].astype(_BF16),
                        w_down[l].astype(_BF16), ln3_g[l], ln3_b[l])
    return x32.reshape(bsz, seq, d)
```

```python
import functools
import math

import jax
import jax.numpy as jnp
from jax import lax
from jax.experimental import pallas as pl
from jax.experimental.pallas import tpu as pltpu

D_MODEL = 1024
DEPTH = 2
HEAD_DIM = 64
ROT_DIM = HEAD_DIM // 4
ROPE_THETA = 500000.0
DIFF_HEADS = 4
DIFF_WIDTH = 512
DIL_HEADS = 8
DIL_WIDTH = 512
DIL_STEPS = 128
DILATIONS = (1, 4, 16)
RNN_WIDTH = D_MODEL
LRU_BLOCK = 64
LRU_C = 8.0
CONV_WIDTH = 4
MEM_TOKENS = 256
XATTN_HEADS = 4
XATTN_HEAD_DIM = D_MODEL // XATTN_HEADS
FFN_HIDDEN = 2816
DEEPNORM_ALPHA = (2 * DEPTH) ** 0.25
LN_EPS = 1e-5

COL_QK_A = 0
COL_RNN = 1536
COL_QK_C = 3584
COL_GATE = 5120

LANES = 128
V7X_VMEM_BYTES = 64 * 1024 * 1024
VMEM_LIMIT = V7X_VMEM_BYTES * 3 // 4

_BF16 = jnp.bfloat16
_F32 = jnp.float32
_NT = (((1,), (1,)), ((), ()))


def _params(*semantics):
    return pltpu.CompilerParams(dimension_semantics=semantics, vmem_limit_bytes=VMEM_LIMIT)


def _layer_norm(z, g, b):
    mu = jnp.mean(z, axis=-1, keepdims=True)
    d = z - mu
    var = jnp.mean(d * d, axis=-1, keepdims=True)
    return d * lax.rsqrt(var + LN_EPS) * g + b


def _cast_kernel(x_ref, o_ref):
    o_ref[...] = x_ref[...].astype(o_ref.dtype)


def _to_bf16(x2, tm=1024):
    n, d = x2.shape
    tm = min(tm, n)
    assert n % tm == 0
    spec = pl.BlockSpec((tm, d), lambda i: (i, 0))
    return pl.pallas_call(_cast_kernel, out_shape=jax.ShapeDtypeStruct((n, d), _BF16), grid=(n // tm,),
                          in_specs=[spec], out_specs=spec, compiler_params=_params("parallel"),
                          name="cast_bf16")(x2)


MXU_WIDTH = 256
GATE_TILE = MXU_WIDTH


def _proj_kernel(x_ref, w_ref, *rest, rotary_width):
    o_ref = rest[-1]
    slabs = len(o_ref.shape) == 3
    x = x_ref[...]
    width = w_ref.shape[1]
    accs = [jnp.dot(x, w_ref[:, c0:c0 + MXU_WIDTH], preferred_element_type=_F32)
            for c0 in range(0, width, MXU_WIDTH)]
    if rotary_width:
        c, s = rest[0][...], rest[1][...]
    per = MXU_WIDTH // LANES
    for g, acc in enumerate(accs):
        for jj in range(per):
            j = g * per + jj
            blk = acc[:, jj * LANES:(jj + 1) * LANES]
            if j * LANES < rotary_width:
                partner = pltpu.roll(blk, LANES // 2, axis=1)
                blk = blk * c + partner * s
            if slabs:
                o_ref[j] = blk.astype(o_ref.dtype)
            else:
                o_ref[:, j * LANES:(j + 1) * LANES] = blk.astype(o_ref.dtype)


_HALF = ROT_DIM // 2
QK_LANE_ORDER = (list(range(0, _HALF)) + list(range(HEAD_DIM, HEAD_DIM + _HALF))
                 + list(range(ROT_DIM, HEAD_DIM))
                 + list(range(_HALF, ROT_DIM)) + list(range(HEAD_DIM + _HALF, HEAD_DIM + ROT_DIM))
                 + list(range(HEAD_DIM + ROT_DIM, 2 * HEAD_DIM)))


def _head0_lanes(shape, axis):
    lane = lax.broadcasted_iota(jnp.int32, shape, axis)
    return (lane < _HALF) | ((lane >= ROT_DIM) & (lane < HEAD_DIM + _HALF))


def _permute_qk_columns(w_qk):
    d, width = w_qk.shape
    grouped = w_qk.reshape(d, width // LANES, LANES)
    return jnp.take(grouped, jnp.asarray(QK_LANE_ORDER, jnp.int32), axis=2).reshape(d, width)


def _rotary_tables(seq):
    inv_freq = jnp.power(ROPE_THETA, -2.0 * jnp.arange(_HALF, dtype=_F32) / ROT_DIM)
    ang = jnp.arange(seq).astype(_F32)[:, None] * inv_freq[None, :]
    cos, sin = jnp.cos(ang), jnp.sin(ang)
    ones = jnp.ones((seq, HEAD_DIM - ROT_DIM), _F32)
    zeros = jnp.zeros((seq, HEAD_DIM - ROT_DIM), _F32)
    c = jnp.concatenate([cos, cos, ones, cos, cos, ones], axis=1)
    s = jnp.concatenate([-sin, -sin, zeros, sin, sin, zeros], axis=1)
    return c, s


def _project(name, x2, w, out_dtype, seq, tables=None, rotary_width=0, slabs=False, tm=1024):
    n, k = x2.shape
    width = w.shape[1]
    tm = min(tm, seq)
    assert n % tm == 0 and seq % tm == 0 and width % MXU_WIDTH == 0 and rotary_width % LANES == 0
    assert (tables is None) == (rotary_width == 0)
    in_specs = [pl.BlockSpec((tm, k), lambda i: (i, 0)), pl.BlockSpec((k, width), lambda i: (0, 0))]
    args = [x2, w]
    if tables is not None:
        in_specs += [pl.BlockSpec((tm, LANES), lambda i: (i % (seq // tm), 0))] * len(tables)
        args += list(tables)
    if slabs:
        out_shape = jax.ShapeDtypeStruct((width // LANES, n, LANES), out_dtype)
        out_spec = pl.BlockSpec((width // LANES, tm, LANES), lambda i: (0, i, 0))
    else:
        out_shape = jax.ShapeDtypeStruct((n, width), out_dtype)
        out_spec = pl.BlockSpec((tm, width), lambda i: (i, 0))
    return pl.pallas_call(
        functools.partial(_proj_kernel, rotary_width=rotary_width),
        out_shape=out_shape,
        grid=(n // tm,),
        in_specs=in_specs,
        out_specs=out_spec,
        compiler_params=_params("parallel"),
        name=name,
    )(*args)


ONES_ROWS = 16
ATTN_SUB = 256


def _column_max(x):
    rows = x.shape[0]
    while rows > 8 and rows % 2 == 0:
        rows //= 2
        x = jnp.maximum(x[:rows], x[rows:])
    return jnp.max(x, axis=0, keepdims=True)


def _diff_attn_kernel(lam_ref, g_ref, q_ref, k_ref, v_ref, o_ref, wq_scr, vt_scr, acc_scr, m_scr,
                      sa_scr, sb_scr, ma_scr, mb_scr, *, lambda_init):
    qi = pl.program_id(2)
    heads, _, t, vd = q_ref.shape
    n_kv = k_ref.shape[2] // t
    span = 2 * t

    @pl.when(qi == 0)
    def _transpose_v():
        for hh in range(heads):
            for j in range(n_kv):
                vt_scr[hh, j, 0:vd, :] = v_ref[hh, 0, j * t:(j + 1) * t, :].astype(_F32).T.astype(_BF16)
                vt_scr[hh, j, vd:vd + ONES_ROWS, :] = jnp.ones((ONES_ROWS, t), _BF16)

    for hh in range(heads):
        q_t = (q_ref[hh, 0].astype(_F32) * (HEAD_DIM ** -0.5)).T
        first = _head0_lanes(q_t.shape, 0)
        wq_scr[:, hh * span:hh * span + t] = jnp.where(first, q_t, 0.0).astype(_BF16)
        wq_scr[:, hh * span + t:(hh + 1) * span] = jnp.where(first, 0.0, q_t).astype(_BF16)
    acc_scr[...] = jnp.zeros_like(acc_scr)
    m_scr[...] = jnp.full_like(m_scr, -jnp.inf)

    chains = [(c * ATTN_SUB // span, slice(c * ATTN_SUB, (c + 1) * ATTN_SUB))
              for c in range(heads * span // ATTN_SUB)]

    def scores(j, buf, chain):
        s_ref, max_ref = buf
        hh, cols = chain
        k_t = k_ref[hh, 0, pl.ds(pl.multiple_of(j * t, t), t), :]
        s = jnp.dot(k_t, wq_scr[:, cols], preferred_element_type=_F32)
        s_ref[:, cols] = s
        max_ref[:, cols] = _column_max(s)

    def softmax_pv(j, buf, chain, masked):
        s_ref, max_ref = buf
        hh, cols = chain
        s = s_ref[:, cols]
        if masked:
            key = lax.broadcasted_iota(jnp.int32, (t, ATTN_SUB), 0)
            qry = lax.broadcasted_iota(jnp.int32, (t, ATTN_SUB), 1) + cols.start % t
            s = jnp.where(key <= qry, s, -jnp.inf)
            m_tile = _column_max(s)
        else:
            m_tile = max_ref[:, cols]
        m_prev = m_scr[:, cols]
        m_new = jnp.maximum(m_prev, m_tile)
        alpha = jnp.exp(m_prev - m_new)
        p = jnp.exp(s - m_new).astype(_BF16)
        acc_scr[:, cols] = alpha * acc_scr[:, cols] + jnp.dot(vt_scr[hh, j], p, preferred_element_type=_F32)
        m_scr[:, cols] = m_new

    def stage(j_next, s_next, j, s_cur, masked):
        for chain in chains:
            if j_next is not None:
                scores(j_next, s_next, chain)
            softmax_pv(j, s_cur, chain, masked)

    buf_a = (sa_scr, ma_scr)
    buf_b = (sb_scr, mb_scr)
    for chain in chains:
        scores(0, buf_a, chain)

    def body(jj, carry):
        j = 2 * jj
        stage(j + 1, buf_b, j, buf_a, False)
        stage(j + 2, buf_a, j + 1, buf_b, False)
        return carry

    lax.fori_loop(0, qi // 2, body, 0)

    @pl.when(qi % 2 == 0)
    def _even_tail():
        stage(None, None, qi, buf_a, True)

    @pl.when(qi % 2 == 1)
    def _odd_tail():
        stage(qi, buf_b, qi - 1, buf_a, False)
        stage(None, None, qi, buf_b, True)

    lq = lam_ref[...]
    lam = (jnp.exp(jnp.sum(lq[0:1] * lq[1:2], axis=1, keepdims=True))
           - jnp.exp(jnp.sum(lq[2:3] * lq[3:4], axis=1, keepdims=True)) + lambda_init)
    for hh in range(heads):
        c1 = slice(hh * span, hh * span + t)
        c2 = slice(hh * span + t, (hh + 1) * span)
        o1 = acc_scr[0:vd, c1] / acc_scr[vd:vd + 1, c1]
        o2 = acc_scr[0:vd, c2] / acc_scr[vd:vd + 1, c2]
        o = (o1 - lam * o2).T
        y = o * lax.rsqrt(jnp.mean(o * o, axis=1, keepdims=True) + LN_EPS) * g_ref[...]
        o_ref[0, :, hh * vd:(hh + 1) * vd] = (y * (1.0 - lambda_init)).astype(o_ref.dtype)


HEAD_GROUP = 2


def _diff_attention(qkv, lam_qk, subln, lambda_init, tile=512):
    _, b, s, vd = qkv.shape
    t = min(tile, s)
    hg = HEAD_GROUP
    assert s % t == 0 and (2 * t) % ATTN_SUB == 0 and t % ATTN_SUB == 0 and vd == 2 * HEAD_DIM
    assert DIFF_HEADS % hg == 0
    groups = DIFF_HEADS // hg
    cols = hg * 2 * t
    return pl.pallas_call(
        functools.partial(_diff_attn_kernel, lambda_init=lambda_init),
        out_shape=jax.ShapeDtypeStruct((b, s, DIFF_WIDTH), _BF16),
        grid=(b, groups, s // t),
        in_specs=[
            pl.BlockSpec((4, HEAD_DIM), lambda bi, h, qi: (0, 0)),
            pl.BlockSpec((1, vd), lambda bi, h, qi: (0, 0)),
            pl.BlockSpec((hg, 1, t, vd), lambda bi, h, qi: (h, bi, qi, 0)),
            pl.BlockSpec((hg, 1, s, vd), lambda bi, h, qi: (groups + h, bi, 0, 0)),
            pl.BlockSpec((hg, 1, s, vd), lambda bi, h, qi: (2 * groups + h, bi, 0, 0)),
        ],
        out_specs=pl.BlockSpec((1, t, hg * vd), lambda bi, h, qi: (bi, qi, h)),
        scratch_shapes=[pltpu.VMEM((vd, cols), _BF16),
                        pltpu.VMEM((hg, s // t, vd + ONES_ROWS, t), _BF16),
                        pltpu.VMEM((vd + ONES_ROWS, cols), _F32),
                        pltpu.VMEM((1, cols), _F32),
                        pltpu.VMEM((t, cols), _F32),
                        pltpu.VMEM((t, cols), _F32),
                        pltpu.VMEM((1, cols), _F32),
                        pltpu.VMEM((1, cols), _F32)],
        compiler_params=_params("parallel", "parallel", "arbitrary"),
        name="diff_attn",
    )(lam_qk, subln.reshape(1, vd), qkv, qkv, qkv)


HIST = 8
LRU_TILE = 128


def _lru_kernel(x_ref, wx_ref, wg_ref, cw_ref, cb_ref, wbd_ref, gb_ref, lam_ref, o_ref,
                xbuf, a_tm, u_tm, g_buf, h_ref):
    nb, t, d = x_ref.shape
    w = RNN_WIDTH
    lane_blocks = w // LANES

    @pl.when(pl.program_id(0) == 0)
    def _init():
        xbuf[:, 0:HIST, :] = jnp.zeros((nb, HIST, w), _F32)
        h_ref[...] = jnp.zeros_like(h_ref)

    x_all = x_ref[...].reshape(nb * t, d)
    xr_all = jnp.dot(x_all, wx_ref[...], preferred_element_type=_F32)
    g_buf[...] = jax.nn.gelu(jnp.dot(x_all, wg_ref[...], preferred_element_type=_F32))
    neg = -lam_ref[...]
    softplus = jnp.maximum(neg, 0.0) + jnp.log1p(jnp.exp(-jnp.abs(neg)))

    for b in range(nb):
        xbuf[b, HIST:HIST + t, :] = xr_all[b * t:(b + 1) * t]
        xc = cb_ref[...]
        for tap in range(CONV_WIDTH):
            start = HIST - (CONV_WIDTH - 1) + tap
            xc = xc + xbuf[b, start:start + t, :] * cw_ref[tap:tap + 1, :]
        xbuf[b, 0:HIST, :] = xbuf[b, t:t + HIST, :]

        xcb = xc.astype(_BF16)
        gates = []
        for g in range(2):
            parts = [jnp.dot(xcb[:, j * GATE_TILE:(j + 1) * GATE_TILE], wbd_ref[g, j],
                             preferred_element_type=_F32) for j in range(w // GATE_TILE)]
            gates.append(jnp.concatenate(parts, axis=1) + gb_ref[g:g + 1, :])
        r = jax.nn.sigmoid(gates[0])
        i = jax.nn.sigmoid(gates[1])
        log_a = -LRU_C * r * softplus
        a = jnp.exp(log_a)
        u = jnp.sqrt(-jnp.tanh(log_a) * (1.0 + a * a)) * (i * xc)
        rows = pl.ds(b, t, stride=nb)
        for lb in range(lane_blocks):
            a_tm[lb, rows, :] = a[:, lb * LANES:(lb + 1) * LANES]
            u_tm[lb, rows, :] = u[:, lb * LANES:(lb + 1) * LANES]

    def step(ts, h):
        rows = pl.ds(pl.multiple_of(ts * nb, nb), nb)
        out = []
        for lb in range(lane_blocks):
            h_lb = a_tm[lb, rows, :] * h[lb] + u_tm[lb, rows, :]
            u_tm[lb, rows, :] = h_lb
            out.append(h_lb)
        return tuple(out)

    h0 = tuple(h_ref[:, lb * LANES:(lb + 1) * LANES] for lb in range(lane_blocks))
    h_end = lax.fori_loop(0, t, step, h0, unroll=8)
    for lb in range(lane_blocks):
        h_ref[:, lb * LANES:(lb + 1) * LANES] = h_end[lb]

    for b in range(nb):
        rows = pl.ds(b, t, stride=nb)
        h_b = jnp.concatenate([u_tm[lb, rows, :] for lb in range(lane_blocks)], axis=1)
        o_ref[b] = (h_b * g_buf[b * t:(b + 1) * t, :]).astype(o_ref.dtype)


def _block_diag_gates(gate_w):
    per = GATE_TILE // LRU_BLOCK
    n_tiles = RNN_WIDTH // GATE_TILE
    gw = gate_w.reshape(2, n_tiles, per, LRU_BLOCK, LRU_BLOCK)
    eye = jnp.eye(per, dtype=gate_w.dtype)
    dense = jnp.einsum("gtpcd,pq->gtpcqd", gw, eye)
    return dense.reshape(2, n_tiles, GATE_TILE, GATE_TILE)


def _rg_lru(x16, w_rnn, conv_w, conv_b, gate_w, gate_b, lru_lambda):
    b, s, d = x16.shape
    t = min(LRU_TILE, s)
    assert s % t == 0
    w = RNN_WIDTH
    full = lambda shape: pl.BlockSpec(shape, lambda ti: (0,) * len(shape))
    return pl.pallas_call(
        _lru_kernel,
        out_shape=jax.ShapeDtypeStruct((b, s, w), _BF16),
        grid=(s // t,),
        in_specs=[pl.BlockSpec((b, t, d), lambda ti: (0, ti, 0)),
                  pl.BlockSpec((d, w), lambda ti: (0, 0)),
                  pl.BlockSpec((d, w), lambda ti: (0, 1)),
                  full((CONV_WIDTH, w)), full((1, w)),
                  full((2, w // GATE_TILE, GATE_TILE, GATE_TILE)), full((2, w)), full((1, w))],
        out_specs=pl.BlockSpec((b, t, w), lambda ti: (0, ti, 0)),
        scratch_shapes=[pltpu.VMEM((b, t + HIST, w), _F32),
                        pltpu.VMEM((w // LANES, t * b, LANES), _F32),
                        pltpu.VMEM((w // LANES, t * b, LANES), _F32),
                        pltpu.VMEM((t * b, w), _F32),
                        pltpu.VMEM((b, w), _F32)],
        compiler_params=_params("arbitrary"),
        name="rg_lru",
    )(x16, w_rnn, w_rnn, conv_w, conv_b.reshape(1, w), _block_diag_gates(gate_w).astype(_BF16), gate_b,
      lru_lambda.reshape(1, w))


DIL_TILE = DILATIONS[-1] * DIL_STEPS
DIL_UNROLL = 4


def _rows(start, size, stride):
    return pl.ds(start, size) if stride == 1 else pl.ds(start, size, stride=stride)


def _dil_kernel(q_ref, k_ref, v_ref, o_ref, qf, kf, vf, o_run, l_run):
    i = pl.program_id(1)
    t = DIL_TILE
    st = DIL_STEPS
    hd = HEAD_DIM

    pairs = DIL_HEADS // 2

    @pl.when(i == 0)
    def _no_history():
        kf[:, 0:t, :] = jnp.zeros((pairs, t, LANES), _F32)
        vf[:, 0:t, :] = jnp.zeros((pairs, t, LANES), _F32)

    @pl.when(i > 0)
    def _shift_history():
        kf[:, 0:t, :] = kf[:, t:2 * t, :]
        vf[:, 0:t, :] = vf[:, t:2 * t, :]

    for pair in range(pairs):
        lanes = slice(pair * LANES, (pair + 1) * LANES)
        qf[pair] = q_ref[0, :, lanes].astype(_F32) * (hd ** -0.5)
        kf[pair, t:2 * t, :] = k_ref[0, :, lanes].astype(_F32)
        vf[pair, t:2 * t, :] = v_ref[0, :, lanes].astype(_F32)

    key = lax.broadcasted_iota(jnp.int32, (2 * st, 2 * st), 0)
    qry = lax.broadcasted_iota(jnp.int32, (2 * st, 2 * st), 1) % st
    dist = st + qry - key
    band = (dist >= 0) & (dist <= st)
    own_block = key >= st
    first_head = _head0_lanes((LANES, st), 0)
    ones = jnp.ones((ONES_ROWS, 2 * st), _F32)

    def block(idx, dil, merge):
        phase = idx % dil
        n = idx // dil
        start = phase + dil * st * n
        q_rows = _rows(start, st, dil)
        kv_rows = _rows(t + start - dil * st, 2 * st, dil)
        valid = band & (own_block | (i > 0) | (n > 0))
        scores = []
        for pair in range(pairs):
            q_t = qf[pair, q_rows, :].T
            k_b = kf[pair, kv_rows, :].astype(_BF16)
            w = jnp.concatenate([jnp.where(first_head, q_t, 0.0),
                                 jnp.where(first_head, 0.0, q_t)], axis=1).astype(_BF16)
            scores.append(jnp.dot(k_b, w, preferred_element_type=_F32))
        o_parts, lse_rows = [], []
        for pair in range(pairs):
            v_t = vf[pair, kv_rows, :].T
            s = jnp.where(valid, scores[pair], -jnp.inf)
            m = _column_max(s)
            p = jnp.exp(s - m).astype(_BF16)
            lhs = jnp.concatenate([v_t[0:hd], ones, v_t[hd:2 * hd], ones], axis=0).astype(_BF16)
            pv = jnp.dot(lhs, p, preferred_element_type=_F32)
            for hh in range(2):
                rr = (hd + ONES_ROWS) * hh
                cc = slice(st * hh, st * (hh + 1))
                l = pv[rr + hd:rr + hd + 1, cc]
                o_parts.append(pv[rr:rr + hd, cc] / l)
                lse_rows.append(m[:, cc] + jnp.log(l))
        lse = jnp.concatenate(lse_rows, axis=0)
        if merge:
            run_l = l_run[q_rows, :].T[0:DIL_HEADS]
            mx = jnp.maximum(run_l, lse)
            w_run = jnp.exp(run_l - mx)
            w_cur = jnp.exp(lse - mx)
            den = w_run + w_cur
            a = w_run / den
            b = w_cur / den
            lse = mx + jnp.log(den)
        outs = []
        for pair in range(pairs):
            halves = o_parts[2 * pair:2 * pair + 2]
            if merge:
                run_o_t = o_run[pair, q_rows, :].T
                halves = [a[h:h + 1] * run_o_t[hh * hd:(hh + 1) * hd] + b[h:h + 1] * halves[hh]
                          for hh, h in enumerate((2 * pair, 2 * pair + 1))]
            outs.append(jnp.concatenate(halves, axis=0).T)
        pad = jnp.zeros((LANES - DIL_HEADS, st), _F32)
        return q_rows, outs, jnp.concatenate([lse, pad], axis=0).T

    def store(q_rows, outs, lse):
        for pair in range(pairs):
            o_run[pair, q_rows, :] = outs[pair]
        l_run[q_rows, :] = lse

    for idx_p, dil in enumerate(reversed(DILATIONS)):
        def body(trip, carry, dil=dil, merge=idx_p > 0):
            results = [block(DIL_UNROLL * trip + u, dil, merge) for u in range(DIL_UNROLL)]
            for res in results:
                store(*res)
            return carry
        lax.fori_loop(0, t // st // DIL_UNROLL, body, 0)

    for pair in range(pairs):
        o_ref[0, :, pair * LANES:(pair + 1) * LANES] = o_run[pair].astype(o_ref.dtype)


def _dilated_attention(qkv):
    b, s, _ = qkv.shape
    t = DIL_TILE
    assert s % t == 0
    return pl.pallas_call(
        _dil_kernel,
        out_shape=jax.ShapeDtypeStruct((b, s, DIL_WIDTH), _BF16),
        grid=(b, s // t),
        in_specs=[pl.BlockSpec((1, t, DIL_WIDTH), lambda bi, i: (bi, i, 0)),
                  pl.BlockSpec((1, t, DIL_WIDTH), lambda bi, i: (bi, i, 1)),
                  pl.BlockSpec((1, t, DIL_WIDTH), lambda bi, i: (bi, i, 2))],
        out_specs=pl.BlockSpec((1, t, DIL_WIDTH), lambda bi, i: (bi, i, 0)),
        scratch_shapes=[pltpu.VMEM((DIL_WIDTH // LANES, t, LANES), _F32),
                        pltpu.VMEM((DIL_WIDTH // LANES, 2 * t, LANES), _F32),
                        pltpu.VMEM((DIL_WIDTH // LANES, 2 * t, LANES), _F32),
                        pltpu.VMEM((DIL_WIDTH // LANES, t, LANES), _F32),
                        pltpu.VMEM((t, LANES), _F32)],
        compiler_params=_params("parallel", "arbitrary"),
        name="dil_attn",
    )(qkv, qkv, qkv)


def _merge_kernel(oa_ref, ob_ref, oc_ref, x16_ref, x_ref, wg_ref, wa_ref, wb_ref, wc_ref, wo_ref,
                  g_ref, b_ref, y32_ref, y16_ref):
    tm = x_ref.shape[0]
    halves = [slice(0, tm // 2), slice(tm // 2, tm)] if tm % 32 == 0 else [slice(0, tm)]
    branches = ((oa_ref, wa_ref), (ob_ref, wb_ref), (oc_ref, wc_ref))
    merged = []
    for rows in halves:
        x16 = x16_ref[rows, :]
        acc = None
        for idx, (o_ref, w_ref) in enumerate(branches):
            gate = jnp.dot(x16, wg_ref[idx], preferred_element_type=_F32)
            branch = jnp.dot(o_ref[rows, :], w_ref[...], preferred_element_type=_F32)
            term = jax.nn.sigmoid(gate) * branch
            acc = term if acc is None else acc + term
        merged.append(acc.astype(_BF16))
    projected = [jnp.dot(m, wo_ref[...], preferred_element_type=_F32) for m in merged]
    for rows, h in zip(halves, projected):
        y = _layer_norm(DEEPNORM_ALPHA * x_ref[rows, :] + h, g_ref[...], b_ref[...])
        y32_ref[rows, :] = y
        y16_ref[rows, :] = y.astype(_BF16)


def _merge(o_a, o_b, o_c, x16, x, w_gates, w_a, w_b, w_c, w_o, g, b, tm=512):
    n, d = x.shape
    tm = min(tm, n)
    assert n % tm == 0
    row = lambda width: pl.BlockSpec((tm, width), lambda i: (i, 0))
    full = lambda shape: pl.BlockSpec(shape, lambda i: (0,) * len(shape), pipeline_mode=pl.Buffered(1))
    return pl.pallas_call(
        _merge_kernel,
        out_shape=(jax.ShapeDtypeStruct((n, d), _F32), jax.ShapeDtypeStruct((n, d), _BF16)),
        grid=(n // tm,),
        in_specs=[row(DIFF_WIDTH), row(RNN_WIDTH), row(DIL_WIDTH), row(d), row(d),
                  full((3, d, d)), full((DIFF_WIDTH, d)), full((RNN_WIDTH, d)), full((DIL_WIDTH, d)),
                  full((d, d)), full((1, d)), full((1, d))],
        out_specs=(row(d), row(d)),
        compiler_params=_params("parallel"),
        name="merge_ln1",
    )(o_a, o_b, o_c, x16, x, w_gates, w_a, w_b, w_c, w_o, g.reshape(1, d), b.reshape(1, d))


def _xattn_kernel(x32_ref, x16_ref, kv_ref, wq_ref, wo_ref, g_ref, b_ref, y32_ref, y16_ref):
    d = D_MODEL
    hd = XATTN_HEAD_DIM
    tm = x16_ref.shape[1]
    halves = [slice(0, tm // 2), slice(tm // 2, tm)] if tm % 16 == 0 else [slice(0, tm)]
    qs = []
    for rows in halves:
        q = jnp.dot(x16_ref[0, rows, :], wq_ref[...], preferred_element_type=_F32)
        qs.append((q * (hd ** -0.5)).astype(_BF16))
    scores = [[lax.dot_general(q[:, h * hd:(h + 1) * hd], kv_ref[0, :, h * hd:(h + 1) * hd], _NT,
                               preferred_element_type=_F32) for h in range(XATTN_HEADS)] for q in qs]
    attended = []
    for half_scores in scores:
        heads = []
        for h, s in enumerate(half_scores):
            v_h = kv_ref[0, :, d + h * hd:d + (h + 1) * hd]
            p = jnp.exp(s - jnp.max(s, axis=1, keepdims=True))
            l = jnp.sum(p, axis=1, keepdims=True)
            heads.append(jnp.dot(p.astype(_BF16), v_h, preferred_element_type=_F32) / l)
        attended.append(jnp.concatenate(heads, axis=1).astype(_BF16))
    projected = [jnp.dot(o, wo_ref[...], preferred_element_type=_F32) for o in attended]
    for rows, h_out in zip(halves, projected):
        y = _layer_norm(DEEPNORM_ALPHA * x32_ref[0, rows, :] + h_out, g_ref[...], b_ref[...])
        y32_ref[0, rows, :] = y
        y16_ref[0, rows, :] = y.astype(_BF16)


def _cross_attention(x32, x16, kv, wq, wo, g, b, tm=512):
    bsz, s, d = x32.shape
    tm = min(tm, s)
    assert s % tm == 0
    row = pl.BlockSpec((1, tm, d), lambda bi, i: (bi, i, 0))
    full = lambda shape: pl.BlockSpec(shape, lambda bi, i: (0, 0))
    return pl.pallas_call(
        _xattn_kernel,
        out_shape=(jax.ShapeDtypeStruct((bsz, s, d), _F32), jax.ShapeDtypeStruct((bsz, s, d), _BF16)),
        grid=(bsz, s // tm),
        in_specs=[row, row, pl.BlockSpec((1, MEM_TOKENS, 2 * d), lambda bi, i: (bi, 0, 0)),
                  full((d, d)), full((d, d)), full((1, d)), full((1, d))],
        out_specs=(row, row),
        compiler_params=_params("parallel", "parallel"),
        name="xattn_ln2",
    )(x32, x16, kv, wq, wo, g.reshape(1, d), b.reshape(1, d))


def _ffn_kernel(x32_ref, x16_ref, wg_ref, wu_ref, wd_ref, g_ref, b_ref, y32_ref, y16_ref, acc_ref):
    c = pl.program_id(1)

    @pl.when(c == 0)
    def _init():
        acc_ref[...] = jnp.zeros_like(acc_ref)

    x = x16_ref[...]
    gate = jnp.dot(x, wg_ref[...], preferred_element_type=_F32)
    up = jnp.dot(x, wu_ref[...], preferred_element_type=_F32)
    hidden = (jax.nn.silu(gate) * up).astype(_BF16)
    acc_ref[...] += jnp.dot(hidden, wd_ref[...], preferred_element_type=_F32)

    @pl.when(c == pl.num_programs(1) - 1)
    def _fin():
        y = _layer_norm(DEEPNORM_ALPHA * x32_ref[...] + acc_ref[...], g_ref[...], b_ref[...])
        y32_ref[...] = y
        y16_ref[...] = y.astype(_BF16)


def _ffn(x32, x16, w_up, w_down, g, b, tm=512, chunks=2):
    n, d = x32.shape
    tm = min(tm, n)
    th = FFN_HIDDEN // chunks
    assert n % tm == 0 and FFN_HIDDEN % chunks == 0 and th % LANES == 0
    row = pl.BlockSpec((tm, d), lambda i, c: (i, 0))
    full = pl.BlockSpec((1, d), lambda i, c: (0, 0))
    return pl.pallas_call(
        _ffn_kernel,
        out_shape=(jax.ShapeDtypeStruct((n, d), _F32), jax.ShapeDtypeStruct((n, d), _BF16)),
        grid=(n // tm, chunks),
        in_specs=[row, row,
                  pl.BlockSpec((d, th), lambda i, c: (0, c)),
                  pl.BlockSpec((d, th), lambda i, c: (0, chunks + c)),
                  pl.BlockSpec((th, d), lambda i, c: (c, 0)),
                  full, full],
        out_specs=(row, row),
        scratch_shapes=[pltpu.VMEM((tm, d), _F32)],
        compiler_params=_params("parallel", "arbitrary"),
        name="ffn_ln3",
    )(x32, x16, w_up, w_up, w_down, g.reshape(1, d), b.reshape(1, d))


def kernel(x, mem, w_in, lam_qk, diff_subln, conv_w, conv_b, gate_w, gate_b, lru_lambda, w_br_a, w_br_b, w_br_c, w_out, ln1_g, ln1_b, xq, xkv, xo, ln2_g, ln2_b, w_up, w_down, ln3_g, ln3_b):
    bsz, seq, d = x.shape
    n = bsz * seq
    tables = _rotary_tables(seq)
    mem16 = mem.reshape(bsz * mem.shape[1], d).astype(_BF16)
    x32 = x.reshape(n, d)
    x16 = _to_bf16(x32)
    for l in range(DEPTH):
        w = w_in[l].astype(_BF16)
        lambda_init = 0.8 - 0.6 * math.exp(-0.3 * l)
        w_qkv_a, w_qkv_c = (
            jnp.concatenate([_permute_qk_columns(w[:, c0:c0 + 1024]), w[:, c0 + 1024:c0 + 1536]], axis=1)
            for c0 in (COL_QK_A, COL_QK_C))
        qkv_a = _project("proj_qkv_a", x16, w_qkv_a, _BF16, seq, tables, rotary_width=1024, slabs=True)
        qkv_c = _project("proj_qkv_c", x16, w_qkv_c, _BF16, seq, tables, rotary_width=1024)
        w_gates = w[:, COL_GATE:COL_GATE + 3 * d].reshape(d, 3, d).transpose(1, 0, 2)

        o_a = _diff_attention(qkv_a.reshape(-1, bsz, seq, LANES), lam_qk[l], diff_subln[l], lambda_init)
        o_b = _rg_lru(x16.reshape(bsz, seq, d), w[:, COL_RNN:COL_RNN + 2 * RNN_WIDTH], conv_w[l], conv_b[l],
                      gate_w[l], gate_b[l], lru_lambda[l])
        o_c = _dilated_attention(qkv_c.reshape(bsz, seq, 1536))

        x32, x16 = _merge(o_a.reshape(n, 512), o_b.reshape(n, 1024), o_c.reshape(n, 512), x16, x32,
                          w_gates, w_br_a[l].astype(_BF16), w_br_b[l].astype(_BF16), w_br_c[l].astype(_BF16),
                          w_out[l].astype(_BF16), ln1_g[l], ln1_b[l])

        kv = _project("proj_mem_kv", mem16, xkv[l].astype(_BF16), _BF16, mem.shape[1])
        x32, x16 = _cross_attention(x32.reshape(bsz, seq, d), x16.reshape(bsz, seq, d),
                                    kv.reshape(bsz, mem.shape[1], 2 * d), xq[l].astype(_BF16),
                                    xo[l].astype(_BF16), ln2_g[l], ln2_b[l])
        x32, x16 = _ffn(x32.reshape(n, d), x16.reshape(n, d), w_up[l].astype(_BF16),
                        w_down[l].astype(_BF16), ln3_g[l], ln3_b[l])
    return x32.reshape(bsz, seq, d)
```

```python
import functools
import math

import jax
import jax.numpy as jnp
from jax import lax
from jax.experimental import pallas as pl
from jax.experimental.pallas import tpu as pltpu

D_MODEL = 1024
DEPTH = 2
HEAD_DIM = 64
ROT_DIM = HEAD_DIM // 4
ROPE_THETA = 500000.0
DIFF_HEADS = 4
DIFF_WIDTH = 512
DIL_HEADS = 8
DIL_WIDTH = 512
DIL_STEPS = 128
DILATIONS = (1, 4, 16)
RNN_WIDTH = D_MODEL
LRU_BLOCK = 64
LRU_C = 8.0
CONV_WIDTH = 4
MEM_TOKENS = 256
XATTN_HEADS = 4
XATTN_HEAD_DIM = D_MODEL // XATTN_HEADS
FFN_HIDDEN = 2816
DEEPNORM_ALPHA = (2 * DEPTH) ** 0.25
LN_EPS = 1e-5

COL_QK_A = 0
COL_RNN = 1536
COL_QK_C = 3584
COL_GATE = 5120

LANES = 128
V7X_VMEM_BYTES = 64 * 1024 * 1024
VMEM_LIMIT = V7X_VMEM_BYTES * 3 // 4

_BF16 = jnp.bfloat16
_F32 = jnp.float32
_NT = (((1,), (1,)), ((), ()))


def _params(*semantics, vmem_limit=VMEM_LIMIT):
    return pltpu.CompilerParams(dimension_semantics=semantics, vmem_limit_bytes=vmem_limit)


def _layer_norm(z, g, b):
    mu = jnp.mean(z, axis=-1, keepdims=True)
    d = z - mu
    var = jnp.mean(d * d, axis=-1, keepdims=True)
    return d * lax.rsqrt(var + LN_EPS) * g + b


def _cast_kernel(x_ref, o_ref):
    o_ref[...] = x_ref[...].astype(o_ref.dtype)


def _to_bf16(x2, tm=1024):
    n, d = x2.shape
    tm = min(tm, n)
    assert n % tm == 0
    spec = pl.BlockSpec((tm, d), lambda i: (i, 0))
    return pl.pallas_call(_cast_kernel, out_shape=jax.ShapeDtypeStruct((n, d), _BF16), grid=(n // tm,),
                          in_specs=[spec], out_specs=spec, compiler_params=_params("parallel"),
                          name="cast_bf16")(x2)


MXU_WIDTH = 256
GATE_TILE = MXU_WIDTH


def _proj_kernel(x_ref, w_ref, *rest, rotary_width):
    o_ref = rest[-1]
    slabs = len(o_ref.shape) == 3
    x = x_ref[...]
    width = w_ref.shape[1]
    accs = [jnp.dot(x, w_ref[:, c0:c0 + MXU_WIDTH], preferred_element_type=_F32)
            for c0 in range(0, width, MXU_WIDTH)]
    if rotary_width:
        c, s = rest[0][...], rest[1][...]
    per = MXU_WIDTH // LANES
    for g, acc in enumerate(accs):
        for jj in range(per):
            j = g * per + jj
            blk = acc[:, jj * LANES:(jj + 1) * LANES]
            if j * LANES < rotary_width:
                partner = pltpu.roll(blk, LANES // 2, axis=1)
                blk = blk * c + partner * s
            if slabs:
                o_ref[j] = blk.astype(o_ref.dtype)
            else:
                o_ref[:, j * LANES:(j + 1) * LANES] = blk.astype(o_ref.dtype)


_HALF = ROT_DIM // 2
QK_LANE_ORDER = (list(range(0, _HALF)) + list(range(HEAD_DIM, HEAD_DIM + _HALF))
                 + list(range(ROT_DIM, HEAD_DIM))
                 + list(range(_HALF, ROT_DIM)) + list(range(HEAD_DIM + _HALF, HEAD_DIM + ROT_DIM))
                 + list(range(HEAD_DIM + ROT_DIM, 2 * HEAD_DIM)))


def _head0_lanes(shape, axis):
    lane = lax.broadcasted_iota(jnp.int32, shape, axis)
    return (lane < _HALF) | ((lane >= ROT_DIM) & (lane < HEAD_DIM + _HALF))


def _permute_qk_columns(w_qk):
    d, width = w_qk.shape
    grouped = w_qk.reshape(d, width // LANES, LANES)
    return jnp.take(grouped, jnp.asarray(QK_LANE_ORDER, jnp.int32), axis=2).reshape(d, width)


def _rotary_tables(seq):
    inv_freq = jnp.power(ROPE_THETA, -2.0 * jnp.arange(_HALF, dtype=_F32) / ROT_DIM)
    ang = jnp.arange(seq).astype(_F32)[:, None] * inv_freq[None, :]
    cos, sin = jnp.cos(ang), jnp.sin(ang)
    ones = jnp.ones((seq, HEAD_DIM - ROT_DIM), _F32)
    zeros = jnp.zeros((seq, HEAD_DIM - ROT_DIM), _F32)
    c = jnp.concatenate([cos, cos, ones, cos, cos, ones], axis=1)
    s = jnp.concatenate([-sin, -sin, zeros, sin, sin, zeros], axis=1)
    return c, s


def _project(name, x2, w, out_dtype, seq, tables=None, rotary_width=0, slabs=False, tm=1024):
    n, k = x2.shape
    width = w.shape[1]
    tm = min(tm, seq)
    assert n % tm == 0 and seq % tm == 0 and width % MXU_WIDTH == 0 and rotary_width % LANES == 0
    assert (tables is None) == (rotary_width == 0)
    in_specs = [pl.BlockSpec((tm, k), lambda i: (i, 0)), pl.BlockSpec((k, width), lambda i: (0, 0))]
    args = [x2, w]
    if tables is not None:
        in_specs += [pl.BlockSpec((tm, LANES), lambda i: (i % (seq // tm), 0))] * len(tables)
        args += list(tables)
    if slabs:
        out_shape = jax.ShapeDtypeStruct((width // LANES, n, LANES), out_dtype)
        out_spec = pl.BlockSpec((width // LANES, tm, LANES), lambda i: (0, i, 0))
    else:
        out_shape = jax.ShapeDtypeStruct((n, width), out_dtype)
        out_spec = pl.BlockSpec((tm, width), lambda i: (i, 0))
    return pl.pallas_call(
        functools.partial(_proj_kernel, rotary_width=rotary_width),
        out_shape=out_shape,
        grid=(n // tm,),
        in_specs=in_specs,
        out_specs=out_spec,
        compiler_params=_params("parallel"),
        name=name,
    )(*args)


ONES_ROWS = 16
ATTN_SUB = 256


def _column_max(x):
    rows = x.shape[0]
    while rows > 8 and rows % 2 == 0:
        rows //= 2
        x = jnp.maximum(x[:rows], x[rows:])
    return jnp.max(x, axis=0, keepdims=True)


def _diff_attn_kernel(lam_ref, g_ref, q_ref, k_ref, v_ref, o_ref, wq_scr, vt_scr, acc_scr, m_scr,
                      sa_scr, sb_scr, ma_scr, mb_scr, *, lambda_init):
    qi = pl.program_id(2)
    heads, _, t, vd = q_ref.shape
    n_kv = k_ref.shape[2] // t
    span = 2 * t

    @pl.when(qi == 0)
    def _transpose_v():
        for hh in range(heads):
            for j in range(n_kv):
                vt_scr[hh, j, 0:vd, :] = v_ref[hh, 0, j * t:(j + 1) * t, :].astype(_F32).T.astype(_BF16)
                vt_scr[hh, j, vd:vd + ONES_ROWS, :] = jnp.ones((ONES_ROWS, t), _BF16)

    for hh in range(heads):
        q_t = (q_ref[hh, 0].astype(_F32) * (HEAD_DIM ** -0.5)).T
        first = _head0_lanes(q_t.shape, 0)
        wq_scr[:, hh * span:hh * span + t] = jnp.where(first, q_t, 0.0).astype(_BF16)
        wq_scr[:, hh * span + t:(hh + 1) * span] = jnp.where(first, 0.0, q_t).astype(_BF16)
    acc_scr[...] = jnp.zeros_like(acc_scr)
    m_scr[...] = jnp.full_like(m_scr, -jnp.inf)

    chains = [(c * ATTN_SUB // span, slice(c * ATTN_SUB, (c + 1) * ATTN_SUB))
              for c in range(heads * span // ATTN_SUB)]

    def scores(j, buf, chain):
        s_ref, max_ref = buf
        hh, cols = chain
        k_t = k_ref[hh, 0, pl.ds(pl.multiple_of(j * t, t), t), :]
        s = jnp.dot(k_t, wq_scr[:, cols], preferred_element_type=_F32)
        s_ref[:, cols] = s
        max_ref[:, cols] = _column_max(s)

    def softmax_pv(j, buf, chain, masked):
        s_ref, max_ref = buf
        hh, cols = chain
        s = s_ref[:, cols]
        if masked:
            key = lax.broadcasted_iota(jnp.int32, (t, ATTN_SUB), 0)
            qry = lax.broadcasted_iota(jnp.int32, (t, ATTN_SUB), 1) + cols.start % t
            s = jnp.where(key <= qry, s, -jnp.inf)
            m_tile = _column_max(s)
        else:
            m_tile = max_ref[:, cols]
        m_prev = m_scr[:, cols]
        m_new = jnp.maximum(m_prev, m_tile)
        alpha = jnp.exp(m_prev - m_new)
        p = jnp.exp(s - m_new).astype(_BF16)
        acc_scr[:, cols] = alpha * acc_scr[:, cols] + jnp.dot(vt_scr[hh, j], p, preferred_element_type=_F32)
        m_scr[:, cols] = m_new

    def stage(j_next, s_next, j, s_cur, masked):
        for chain in chains:
            if j_next is not None:
                scores(j_next, s_next, chain)
            softmax_pv(j, s_cur, chain, masked)

    buf_a = (sa_scr, ma_scr)
    buf_b = (sb_scr, mb_scr)
    for chain in chains:
        scores(0, buf_a, chain)

    def body(jj, carry):
        j = 2 * jj
        stage(j + 1, buf_b, j, buf_a, False)
        stage(j + 2, buf_a, j + 1, buf_b, False)
        return carry

    lax.fori_loop(0, qi // 2, body, 0)

    @pl.when(qi % 2 == 0)
    def _even_tail():
        stage(None, None, qi, buf_a, True)

    @pl.when(qi % 2 == 1)
    def _odd_tail():
        stage(qi, buf_b, qi - 1, buf_a, False)
        stage(None, None, qi, buf_b, True)

    lq = lam_ref[...]
    lam = (jnp.exp(jnp.sum(lq[0:1] * lq[1:2], axis=1, keepdims=True))
           - jnp.exp(jnp.sum(lq[2:3] * lq[3:4], axis=1, keepdims=True)) + lambda_init)
    for hh in range(heads):
        c1 = slice(hh * span, hh * span + t)
        c2 = slice(hh * span + t, (hh + 1) * span)
        o1 = acc_scr[0:vd, c1] / acc_scr[vd:vd + 1, c1]
        o2 = acc_scr[0:vd, c2] / acc_scr[vd:vd + 1, c2]
        o = (o1 - lam * o2).T
        y = o * lax.rsqrt(jnp.mean(o * o, axis=1, keepdims=True) + LN_EPS) * g_ref[...]
        o_ref[0, :, hh * vd:(hh + 1) * vd] = (y * (1.0 - lambda_init)).astype(o_ref.dtype)


HEAD_GROUP = 4


def _diff_attention(qkv, lam_qk, subln, lambda_init, tile=512):
    _, b, s, vd = qkv.shape
    t = min(tile, s)
    hg = HEAD_GROUP
    assert s % t == 0 and (2 * t) % ATTN_SUB == 0 and t % ATTN_SUB == 0 and vd == 2 * HEAD_DIM
    assert DIFF_HEADS % hg == 0
    groups = DIFF_HEADS // hg
    cols = hg * 2 * t
    return pl.pallas_call(
        functools.partial(_diff_attn_kernel, lambda_init=lambda_init),
        out_shape=jax.ShapeDtypeStruct((b, s, DIFF_WIDTH), _BF16),
        grid=(b, groups, s // t),
        in_specs=[
            pl.BlockSpec((4, HEAD_DIM), lambda bi, h, qi: (0, 0)),
            pl.BlockSpec((1, vd), lambda bi, h, qi: (0, 0)),
            pl.BlockSpec((hg, 1, t, vd), lambda bi, h, qi: (h, bi, qi, 0)),
            pl.BlockSpec((hg, 1, s, vd), lambda bi, h, qi: (groups + h, bi, 0, 0),
                         pipeline_mode=pl.Buffered(1)),
            pl.BlockSpec((hg, 1, s, vd), lambda bi, h, qi: (2 * groups + h, bi, 0, 0),
                         pipeline_mode=pl.Buffered(1)),
        ],
        out_specs=pl.BlockSpec((1, t, hg * vd), lambda bi, h, qi: (bi, qi, h)),
        scratch_shapes=[pltpu.VMEM((vd, cols), _BF16),
                        pltpu.VMEM((hg, s // t, vd + ONES_ROWS, t), _BF16),
                        pltpu.VMEM((vd + ONES_ROWS, cols), _F32),
                        pltpu.VMEM((1, cols), _F32),
                        pltpu.VMEM((t, cols), _F32),
                        pltpu.VMEM((t, cols), _F32),
                        pltpu.VMEM((1, cols), _F32),
                        pltpu.VMEM((1, cols), _F32)],
        compiler_params=_params("parallel", "parallel", "arbitrary", vmem_limit=V7X_VMEM_BYTES * 7 // 8),
        name="diff_attn",
    )(lam_qk, subln.reshape(1, vd), qkv, qkv, qkv)


HIST = 8
LRU_TILE = 128


def _lru_kernel(x_ref, wx_ref, wg_ref, cw_ref, cb_ref, wbd_ref, gb_ref, lam_ref, o_ref,
                xbuf, a_tm, u_tm, g_buf, h_ref):
    nb, t, d = x_ref.shape
    w = RNN_WIDTH
    lane_blocks = w // LANES

    @pl.when(pl.program_id(0) == 0)
    def _init():
        xbuf[:, 0:HIST, :] = jnp.zeros((nb, HIST, w), _F32)
        h_ref[...] = jnp.zeros_like(h_ref)

    x_all = x_ref[...].reshape(nb * t, d)
    xr_all = jnp.dot(x_all, wx_ref[...], preferred_element_type=_F32)
    g_buf[...] = jax.nn.gelu(jnp.dot(x_all, wg_ref[...], preferred_element_type=_F32))
    neg = -lam_ref[...]
    softplus = jnp.maximum(neg, 0.0) + jnp.log1p(jnp.exp(-jnp.abs(neg)))

    for b in range(nb):
        xbuf[b, HIST:HIST + t, :] = xr_all[b * t:(b + 1) * t]
        xc = cb_ref[...]
        for tap in range(CONV_WIDTH):
            start = HIST - (CONV_WIDTH - 1) + tap
            xc = xc + xbuf[b, start:start + t, :] * cw_ref[tap:tap + 1, :]
        xbuf[b, 0:HIST, :] = xbuf[b, t:t + HIST, :]

        xcb = xc.astype(_BF16)
        gates = []
        for g in range(2):
            parts = [jnp.dot(xcb[:, j * GATE_TILE:(j + 1) * GATE_TILE], wbd_ref[g, j],
                             preferred_element_type=_F32) for j in range(w // GATE_TILE)]
            gates.append(jnp.concatenate(parts, axis=1) + gb_ref[g:g + 1, :])
        r = jax.nn.sigmoid(gates[0])
        i = jax.nn.sigmoid(gates[1])
        log_a = -LRU_C * r * softplus
        a = jnp.exp(log_a)
        u = jnp.sqrt(-jnp.tanh(log_a) * (1.0 + a * a)) * (i * xc)
        rows = pl.ds(b, t, stride=nb)
        for lb in range(lane_blocks):
            a_tm[lb, rows, :] = a[:, lb * LANES:(lb + 1) * LANES]
            u_tm[lb, rows, :] = u[:, lb * LANES:(lb + 1) * LANES]

    def step(ts, h):
        rows = pl.ds(pl.multiple_of(ts * nb, nb), nb)
        out = []
        for lb in range(lane_blocks):
            h_lb = a_tm[lb, rows, :] * h[lb] + u_tm[lb, rows, :]
            u_tm[lb, rows, :] = h_lb
            out.append(h_lb)
        return tuple(out)

    h0 = tuple(h_ref[:, lb * LANES:(lb + 1) * LANES] for lb in range(lane_blocks))
    h_end = lax.fori_loop(0, t, step, h0, unroll=8)
    for lb in range(lane_blocks):
        h_ref[:, lb * LANES:(lb + 1) * LANES] = h_end[lb]

    for b in range(nb):
        rows = pl.ds(b, t, stride=nb)
        h_b = jnp.concatenate([u_tm[lb, rows, :] for lb in range(lane_blocks)], axis=1)
        o_ref[b] = (h_b * g_buf[b * t:(b + 1) * t, :]).astype(o_ref.dtype)


def _block_diag_gates(gate_w):
    per = GATE_TILE // LRU_BLOCK
    n_tiles = RNN_WIDTH // GATE_TILE
    gw = gate_w.reshape(2, n_tiles, per, LRU_BLOCK, LRU_BLOCK)
    eye = jnp.eye(per, dtype=gate_w.dtype)
    dense = jnp.einsum("gtpcd,pq->gtpcqd", gw, eye)
    return dense.reshape(2, n_tiles, GATE_TILE, GATE_TILE)


def _rg_lru(x16, w_rnn, conv_w, conv_b, gate_w, gate_b, lru_lambda):
    b, s, d = x16.shape
    t = min(LRU_TILE, s)
    assert s % t == 0
    w = RNN_WIDTH
    full = lambda shape: pl.BlockSpec(shape, lambda ti: (0,) * len(shape))
    return pl.pallas_call(
        _lru_kernel,
        out_shape=jax.ShapeDtypeStruct((b, s, w), _BF16),
        grid=(s // t,),
        in_specs=[pl.BlockSpec((b, t, d), lambda ti: (0, ti, 0)),
                  pl.BlockSpec((d, w), lambda ti: (0, 0)),
                  pl.BlockSpec((d, w), lambda ti: (0, 1)),
                  full((CONV_WIDTH, w)), full((1, w)),
                  full((2, w // GATE_TILE, GATE_TILE, GATE_TILE)), full((2, w)), full((1, w))],
        out_specs=pl.BlockSpec((b, t, w), lambda ti: (0, ti, 0)),
        scratch_shapes=[pltpu.VMEM((b, t + HIST, w), _F32),
                        pltpu.VMEM((w // LANES, t * b, LANES), _F32),
                        pltpu.VMEM((w // LANES, t * b, LANES), _F32),
                        pltpu.VMEM((t * b, w), _F32),
                        pltpu.VMEM((b, w), _F32)],
        compiler_params=_params("arbitrary"),
        name="rg_lru",
    )(x16, w_rnn, w_rnn, conv_w, conv_b.reshape(1, w), _block_diag_gates(gate_w).astype(_BF16), gate_b,
      lru_lambda.reshape(1, w))


DIL_TILE = DILATIONS[-1] * DIL_STEPS
DIL_UNROLL = 4


def _rows(start, size, stride):
    return pl.ds(start, size) if stride == 1 else pl.ds(start, size, stride=stride)


def _dil_kernel(q_ref, k_ref, v_ref, o_ref, qf, kf, vf, o_run, l_run):
    i = pl.program_id(1)
    t = DIL_TILE
    st = DIL_STEPS
    hd = HEAD_DIM

    pairs = DIL_HEADS // 2

    @pl.when(i == 0)
    def _no_history():
        kf[:, 0:t, :] = jnp.zeros((pairs, t, LANES), _F32)
        vf[:, 0:t, :] = jnp.zeros((pairs, t, LANES), _F32)

    @pl.when(i > 0)
    def _shift_history():
        kf[:, 0:t, :] = kf[:, t:2 * t, :]
        vf[:, 0:t, :] = vf[:, t:2 * t, :]

    for pair in range(pairs):
        lanes = slice(pair * LANES, (pair + 1) * LANES)
        qf[pair] = q_ref[0, :, lanes].astype(_F32) * (hd ** -0.5)
        kf[pair, t:2 * t, :] = k_ref[0, :, lanes].astype(_F32)
        vf[pair, t:2 * t, :] = v_ref[0, :, lanes].astype(_F32)

    key = lax.broadcasted_iota(jnp.int32, (2 * st, 2 * st), 0)
    qry = lax.broadcasted_iota(jnp.int32, (2 * st, 2 * st), 1) % st
    dist = st + qry - key
    band = (dist >= 0) & (dist <= st)
    own_block = key >= st
    first_head = _head0_lanes((LANES, st), 0)
    ones = jnp.ones((ONES_ROWS, 2 * st), _F32)

    def block(idx, dil, merge):
        phase = idx % dil
        n = idx // dil
        start = phase + dil * st * n
        q_rows = _rows(start, st, dil)
        kv_rows = _rows(t + start - dil * st, 2 * st, dil)
        valid = band & (own_block | (i > 0) | (n > 0))
        scores = []
        for pair in range(pairs):
            q_t = qf[pair, q_rows, :].T
            k_b = kf[pair, kv_rows, :].astype(_BF16)
            w = jnp.concatenate([jnp.where(first_head, q_t, 0.0),
                                 jnp.where(first_head, 0.0, q_t)], axis=1).astype(_BF16)
            scores.append(jnp.dot(k_b, w, preferred_element_type=_F32))
        o_parts, lse_rows = [], []
        for pair in range(pairs):
            v_t = vf[pair, kv_rows, :].T
            s = jnp.where(valid, scores[pair], -jnp.inf)
            m = _column_max(s)
            p = jnp.exp(s - m).astype(_BF16)
            lhs = jnp.concatenate([v_t[0:hd], ones, v_t[hd:2 * hd], ones], axis=0).astype(_BF16)
            pv = jnp.dot(lhs, p, preferred_element_type=_F32)
            for hh in range(2):
                rr = (hd + ONES_ROWS) * hh
                cc = slice(st * hh, st * (hh + 1))
                l = pv[rr + hd:rr + hd + 1, cc]
                o_parts.append(pv[rr:rr + hd, cc] / l)
                lse_rows.append(m[:, cc] + jnp.log(l))
        lse = jnp.concatenate(lse_rows, axis=0)
        if merge:
            run_l = l_run[q_rows, :].T[0:DIL_HEADS]
            mx = jnp.maximum(run_l, lse)
            w_run = jnp.exp(run_l - mx)
            w_cur = jnp.exp(lse - mx)
            den = w_run + w_cur
            a = w_run / den
            b = w_cur / den
            lse = mx + jnp.log(den)
        outs = []
        for pair in range(pairs):
            halves = o_parts[2 * pair:2 * pair + 2]
            if merge:
                run_o_t = o_run[pair, q_rows, :].T
                halves = [a[h:h + 1] * run_o_t[hh * hd:(hh + 1) * hd] + b[h:h + 1] * halves[hh]
                          for hh, h in enumerate((2 * pair, 2 * pair + 1))]
            outs.append(jnp.concatenate(halves, axis=0).T)
        pad = jnp.zeros((LANES - DIL_HEADS, st), _F32)
        return q_rows, outs, jnp.concatenate([lse, pad], axis=0).T

    def store(q_rows, outs, lse):
        for pair in range(pairs):
            o_run[pair, q_rows, :] = outs[pair]
        l_run[q_rows, :] = lse

    for idx_p, dil in enumerate(reversed(DILATIONS)):
        def body(trip, carry, dil=dil, merge=idx_p > 0):
            results = [block(DIL_UNROLL * trip + u, dil, merge) for u in range(DIL_UNROLL)]
            for res in results:
                store(*res)
            return carry
        lax.fori_loop(0, t // st // DIL_UNROLL, body, 0)

    for pair in range(pairs):
        o_ref[0, :, pair * LANES:(pair + 1) * LANES] = o_run[pair].astype(o_ref.dtype)


def _dilated_attention(qkv):
    b, s, _ = qkv.shape
    t = DIL_TILE
    assert s % t == 0
    return pl.pallas_call(
        _dil_kernel,
        out_shape=jax.ShapeDtypeStruct((b, s, DIL_WIDTH), _BF16),
        grid=(b, s // t),
        in_specs=[pl.BlockSpec((1, t, DIL_WIDTH), lambda bi, i: (bi, i, 0)),
                  pl.BlockSpec((1, t, DIL_WIDTH), lambda bi, i: (bi, i, 1)),
                  pl.BlockSpec((1, t, DIL_WIDTH), lambda bi, i: (bi, i, 2))],
        out_specs=pl.BlockSpec((1, t, DIL_WIDTH), lambda bi, i: (bi, i, 0)),
        scratch_shapes=[pltpu.VMEM((DIL_WIDTH // LANES, t, LANES), _F32),
                        pltpu.VMEM((DIL_WIDTH // LANES, 2 * t, LANES), _F32),
                        pltpu.VMEM((DIL_WIDTH // LANES, 2 * t, LANES), _F32),
                        pltpu.VMEM((DIL_WIDTH // LANES, t, LANES), _F32),
                        pltpu.VMEM((t, LANES), _F32)],
        compiler_params=_params("parallel", "arbitrary"),
        name="dil_attn",
    )(qkv, qkv, qkv)


def _merge_kernel(oa_ref, ob_ref, oc_ref, x16_ref, x_ref, wg_ref, wa_ref, wb_ref, wc_ref, wo_ref,
                  g_ref, b_ref, y32_ref, y16_ref):
    tm = x_ref.shape[0]
    halves = [slice(0, tm // 2), slice(tm // 2, tm)] if tm % 32 == 0 else [slice(0, tm)]
    branches = ((oa_ref, wa_ref), (ob_ref, wb_ref), (oc_ref, wc_ref))
    merged = []
    for rows in halves:
        x16 = x16_ref[rows, :]
        acc = None
        for idx, (o_ref, w_ref) in enumerate(branches):
            gate = jnp.dot(x16, wg_ref[idx], preferred_element_type=_F32)
            branch = jnp.dot(o_ref[rows, :], w_ref[...], preferred_element_type=_F32)
            term = jax.nn.sigmoid(gate) * branch
            acc = term if acc is None else acc + term
        merged.append(acc.astype(_BF16))
    projected = [jnp.dot(m, wo_ref[...], preferred_element_type=_F32) for m in merged]
    for rows, h in zip(halves, projected):
        y = _layer_norm(DEEPNORM_ALPHA * x_ref[rows, :] + h, g_ref[...], b_ref[...])
        y32_ref[rows, :] = y
        y16_ref[rows, :] = y.astype(_BF16)


def _merge(o_a, o_b, o_c, x16, x, w_gates, w_a, w_b, w_c, w_o, g, b, tm=512):
    n, d = x.shape
    tm = min(tm, n)
    assert n % tm == 0
    row = lambda width: pl.BlockSpec((tm, width), lambda i: (i, 0))
    full = lambda shape: pl.BlockSpec(shape, lambda i: (0,) * len(shape), pipeline_mode=pl.Buffered(1))
    return pl.pallas_call(
        _merge_kernel,
        out_shape=(jax.ShapeDtypeStruct((n, d), _F32), jax.ShapeDtypeStruct((n, d), _BF16)),
        grid=(n // tm,),
        in_specs=[row(DIFF_WIDTH), row(RNN_WIDTH), row(DIL_WIDTH), row(d), row(d),
                  full((3, d, d)), full((DIFF_WIDTH, d)), full((RNN_WIDTH, d)), full((DIL_WIDTH, d)),
                  full((d, d)), full((1, d)), full((1, d))],
        out_specs=(row(d), row(d)),
        compiler_params=_params("parallel"),
        name="merge_ln1",
    )(o_a, o_b, o_c, x16, x, w_gates, w_a, w_b, w_c, w_o, g.reshape(1, d), b.reshape(1, d))


def _xattn_kernel(x32_ref, x16_ref, kv_ref, wq_ref, wo_ref, g_ref, b_ref, y32_ref, y16_ref):
    d = D_MODEL
    hd = XATTN_HEAD_DIM
    tm = x16_ref.shape[1]
    halves = [slice(0, tm // 2), slice(tm // 2, tm)] if tm % 16 == 0 else [slice(0, tm)]
    qs = []
    for rows in halves:
        q = jnp.dot(x16_ref[0, rows, :], wq_ref[...], preferred_element_type=_F32)
        qs.append((q * (hd ** -0.5)).astype(_BF16))
    scores = [[lax.dot_general(q[:, h * hd:(h + 1) * hd], kv_ref[0, :, h * hd:(h + 1) * hd], _NT,
                               preferred_element_type=_F32) for h in range(XATTN_HEADS)] for q in qs]
    attended = []
    for half_scores in scores:
        heads = []
        for h, s in enumerate(half_scores):
            v_h = kv_ref[0, :, d + h * hd:d + (h + 1) * hd]
            p = jnp.exp(s - jnp.max(s, axis=1, keepdims=True))
            l = jnp.sum(p, axis=1, keepdims=True)
            heads.append(jnp.dot(p.astype(_BF16), v_h, preferred_element_type=_F32) / l)
        attended.append(jnp.concatenate(heads, axis=1).astype(_BF16))
    projected = [jnp.dot(o, wo_ref[...], preferred_element_type=_F32) for o in attended]
    for rows, h_out in zip(halves, projected):
        y = _layer_norm(DEEPNORM_ALPHA * x32_ref[0, rows, :] + h_out, g_ref[...], b_ref[...])
        y32_ref[0, rows, :] = y
        y16_ref[0, rows, :] = y.astype(_BF16)


def _cross_attention(x32, x16, kv, wq, wo, g, b, tm=512):
    bsz, s, d = x32.shape
    tm = min(tm, s)
    assert s % tm == 0
    row = pl.BlockSpec((1, tm, d), lambda bi, i: (bi, i, 0))
    full = lambda shape: pl.BlockSpec(shape, lambda bi, i: (0, 0))
    return pl.pallas_call(
        _xattn_kernel,
        out_shape=(jax.ShapeDtypeStruct((bsz, s, d), _F32), jax.ShapeDtypeStruct((bsz, s, d), _BF16)),
        grid=(bsz, s // tm),
        in_specs=[row, row, pl.BlockSpec((1, MEM_TOKENS, 2 * d), lambda bi, i: (bi, 0, 0)),
                  full((d, d)), full((d, d)), full((1, d)), full((1, d))],
        out_specs=(row, row),
        compiler_params=_params("parallel", "parallel"),
        name="xattn_ln2",
    )(x32, x16, kv, wq, wo, g.reshape(1, d), b.reshape(1, d))


def _ffn_kernel(x32_ref, x16_ref, wg_ref, wu_ref, wd_ref, g_ref, b_ref, y32_ref, y16_ref, acc_ref):
    c = pl.program_id(1)

    @pl.when(c == 0)
    def _init():
        acc_ref[...] = jnp.zeros_like(acc_ref)

    x = x16_ref[...]
    gate = jnp.dot(x, wg_ref[...], preferred_element_type=_F32)
    up = jnp.dot(x, wu_ref[...], preferred_element_type=_F32)
    hidden = (jax.nn.silu(gate) * up).astype(_BF16)
    acc_ref[...] += jnp.dot(hidden, wd_ref[...], preferred_element_type=_F32)

    @pl.when(c == pl.num_programs(1) - 1)
    def _fin():
        y = _layer_norm(DEEPNORM_ALPHA * x32_ref[...] + acc_ref[...], g_ref[...], b_ref[...])
        y32_ref[...] = y
        y16_ref[...] = y.astype(_BF16)


def _ffn(x32, x16, w_up, w_down, g, b, tm=512, chunks=2):
    n, d = x32.shape
    tm = min(tm, n)
    th = FFN_HIDDEN // chunks
    assert n % tm == 0 and FFN_HIDDEN % chunks == 0 and th % LANES == 0
    row = pl.BlockSpec((tm, d), lambda i, c: (i, 0))
    full = pl.BlockSpec((1, d), lambda i, c: (0, 0))
    return pl.pallas_call(
        _ffn_kernel,
        out_shape=(jax.ShapeDtypeStruct((n, d), _F32), jax.ShapeDtypeStruct((n, d), _BF16)),
        grid=(n // tm, chunks),
        in_specs=[row, row,
                  pl.BlockSpec((d, th), lambda i, c: (0, c)),
                  pl.BlockSpec((d, th), lambda i, c: (0, chunks + c)),
                  pl.BlockSpec((th, d), lambda i, c: (c, 0)),
                  full, full],
        out_specs=(row, row),
        scratch_shapes=[pltpu.VMEM((tm, d), _F32)],
        compiler_params=_params("parallel", "arbitrary"),
        name="ffn_ln3",
    )(x32, x16, w_up, w_up, w_down, g.reshape(1, d), b.reshape(1, d))


def kernel(x, mem, w_in, lam_qk, diff_subln, conv_w, conv_b, gate_w, gate_b, lru_lambda, w_br_a, w_br_b, w_br_c, w_out, ln1_g, ln1_b, xq, xkv, xo, ln2_g, ln2_b, w_up, w_down, ln3_g, ln3_b):
    bsz, seq, d = x.shape
    n = bsz * seq
    tables = _rotary_tables(seq)
    mem16 = mem.reshape(bsz * mem.shape[1], d).astype(_BF16)
    x32 = x.reshape(n, d)
    x16 = _to_bf16(x32)
    for l in range(DEPTH):
        w = w_in[l].astype(_BF16)
        lambda_init = 0.8 - 0.6 * math.exp(-0.3 * l)
        w_qkv_a, w_qkv_c = (
            jnp.concatenate([_permute_qk_columns(w[:, c0:c0 + 1024]), w[:, c0 + 1024:c0 + 1536]], axis=1)
            for c0 in (COL_QK_A, COL_QK_C))
        qkv_a = _project("proj_qkv_a", x16, w_qkv_a, _BF16, seq, tables, rotary_width=1024, slabs=True)
        qkv_c = _project("proj_qkv_c", x16, w_qkv_c, _BF16, seq, tables, rotary_width=1024)
        w_gates = w[:, COL_GATE:COL_GATE + 3 * d].reshape(d, 3, d).transpose(1, 0, 2)

        o_a = _diff_attention(qkv_a.reshape(-1, bsz, seq, LANES), lam_qk[l], diff_subln[l], lambda_init)
        o_b = _rg_lru(x16.reshape(bsz, seq, d), w[:, COL_RNN:COL_RNN + 2 * RNN_WIDTH], conv_w[l], conv_b[l],
                      gate_w[l], gate_b[l], lru_lambda[l])
        o_c = _dilated_attention(qkv_c.reshape(bsz, seq, 1536))

        x32, x16 = _merge(o_a.reshape(n, 512), o_b.reshape(n, 1024), o_c.reshape(n, 512), x16, x32,
                          w_gates, w_br_a[l].astype(_BF16), w_br_b[l].astype(_BF16), w_br_c[l].astype(_BF16),
                          w_out[l].astype(_BF16), ln1_g[l], ln1_b[l])

        kv = _project("proj_mem_kv", mem16, xkv[l].astype(_BF16), _BF16, mem.shape[1])
        x32, x16 = _cross_attention(x32.reshape(bsz, seq, d), x16.reshape(bsz, seq, d),
                                    kv.reshape(bsz, mem.shape[1], 2 * d), xq[l].astype(_BF16),
                                    xo[l].astype(_BF16), ln2_g[l], ln2_b[l])
        x32, x16 = _ffn(x32.reshape(n, d), x16.reshape(n, d), w_up[l].astype(_BF16),
                        w_down[l].astype(_BF16), ln3_g[l], ln3_b[l])
    return x32.reshape(bsz, seq, d)
```

```python
import functools
import math

import jax
import jax.numpy as jnp
from jax import lax
from jax.experimental import pallas as pl
from jax.experimental.pallas import tpu as pltpu

D_MODEL = 1024
DEPTH = 2
HEAD_DIM = 64
ROT_DIM = HEAD_DIM // 4
ROPE_THETA = 500000.0
DIFF_HEADS = 4
DIFF_WIDTH = 512
DIL_HEADS = 8
DIL_WIDTH = 512
DIL_STEPS = 128
DILATIONS = (1, 4, 16)
RNN_WIDTH = D_MODEL
LRU_BLOCK = 64
LRU_C = 8.0
CONV_WIDTH = 4
MEM_TOKENS = 256
XATTN_HEADS = 4
XATTN_HEAD_DIM = D_MODEL // XATTN_HEADS
FFN_HIDDEN = 2816
DEEPNORM_ALPHA = (2 * DEPTH) ** 0.25
LN_EPS = 1e-5

COL_QK_A = 0
COL_RNN = 1536
COL_QK_C = 3584
COL_GATE = 5120

LANES = 128
V7X_VMEM_BYTES = 64 * 1024 * 1024
VMEM_LIMIT = V7X_VMEM_BYTES * 3 // 4

_BF16 = jnp.bfloat16
_F32 = jnp.float32
_NT = (((1,), (1,)), ((), ()))


def _params(*semantics, vmem_limit=VMEM_LIMIT):
    return pltpu.CompilerParams(dimension_semantics=semantics, vmem_limit_bytes=vmem_limit)


def _layer_norm(z, g, b):
    mu = jnp.mean(z, axis=-1, keepdims=True)
    d = z - mu
    var = jnp.mean(d * d, axis=-1, keepdims=True)
    return d * lax.rsqrt(var + LN_EPS) * g + b


def _cast_kernel(x_ref, o_ref):
    o_ref[...] = x_ref[...].astype(o_ref.dtype)


def _to_bf16(x2, tm=1024):
    n, d = x2.shape
    tm = min(tm, n)
    assert n % tm == 0
    spec = pl.BlockSpec((tm, d), lambda i: (i, 0))
    return pl.pallas_call(_cast_kernel, out_shape=jax.ShapeDtypeStruct((n, d), _BF16), grid=(n // tm,),
                          in_specs=[spec], out_specs=spec, compiler_params=_params("parallel"),
                          name="cast_bf16")(x2)


MXU_WIDTH = 256
GATE_TILE = MXU_WIDTH


def _proj_kernel(x_ref, w_ref, *rest, rotary_width):
    o_ref = rest[-1]
    slabs = len(o_ref.shape) == 3
    x = x_ref[...]
    width = w_ref.shape[1]
    accs = [jnp.dot(x, w_ref[:, c0:c0 + MXU_WIDTH], preferred_element_type=_F32)
            for c0 in range(0, width, MXU_WIDTH)]
    if rotary_width:
        c, s = rest[0][...], rest[1][...]
    per = MXU_WIDTH // LANES
    for g, acc in enumerate(accs):
        for jj in range(per):
            j = g * per + jj
            blk = acc[:, jj * LANES:(jj + 1) * LANES]
            if j * LANES < rotary_width:
                partner = pltpu.roll(blk, LANES // 2, axis=1)
                blk = blk * c + partner * s
            if slabs:
                o_ref[j] = blk.astype(o_ref.dtype)
            else:
                o_ref[:, j * LANES:(j + 1) * LANES] = blk.astype(o_ref.dtype)


_HALF = ROT_DIM // 2
QK_LANE_ORDER = (list(range(0, _HALF)) + list(range(HEAD_DIM, HEAD_DIM + _HALF))
                 + list(range(ROT_DIM, HEAD_DIM))
                 + list(range(_HALF, ROT_DIM)) + list(range(HEAD_DIM + _HALF, HEAD_DIM + ROT_DIM))
                 + list(range(HEAD_DIM + ROT_DIM, 2 * HEAD_DIM)))


def _head0_lanes(shape, axis):
    lane = lax.broadcasted_iota(jnp.int32, shape, axis)
    return (lane < _HALF) | ((lane >= ROT_DIM) & (lane < HEAD_DIM + _HALF))


def _permute_qk_columns(w_qk):
    d, width = w_qk.shape
    grouped = w_qk.reshape(d, width // LANES, LANES)
    return jnp.take(grouped, jnp.asarray(QK_LANE_ORDER, jnp.int32), axis=2).reshape(d, width)


def _rotary_tables(seq):
    inv_freq = jnp.power(ROPE_THETA, -2.0 * jnp.arange(_HALF, dtype=_F32) / ROT_DIM)
    ang = jnp.arange(seq).astype(_F32)[:, None] * inv_freq[None, :]
    cos, sin = jnp.cos(ang), jnp.sin(ang)
    ones = jnp.ones((seq, HEAD_DIM - ROT_DIM), _F32)
    zeros = jnp.zeros((seq, HEAD_DIM - ROT_DIM), _F32)
    c = jnp.concatenate([cos, cos, ones, cos, cos, ones], axis=1)
    s = jnp.concatenate([-sin, -sin, zeros, sin, sin, zeros], axis=1)
    return c, s


def _project(name, x2, w, out_dtype, seq, tables=None, rotary_width=0, slabs=False, tm=1024):
    n, k = x2.shape
    width = w.shape[1]
    tm = min(tm, seq)
    assert n % tm == 0 and seq % tm == 0 and width % MXU_WIDTH == 0 and rotary_width % LANES == 0
    assert (tables is None) == (rotary_width == 0)
    in_specs = [pl.BlockSpec((tm, k), lambda i: (i, 0)), pl.BlockSpec((k, width), lambda i: (0, 0))]
    args = [x2, w]
    if tables is not None:
        in_specs += [pl.BlockSpec((tm, LANES), lambda i: (i % (seq // tm), 0))] * len(tables)
        args += list(tables)
    if slabs:
        out_shape = jax.ShapeDtypeStruct((width // LANES, n, LANES), out_dtype)
        out_spec = pl.BlockSpec((width // LANES, tm, LANES), lambda i: (0, i, 0))
    else:
        out_shape = jax.ShapeDtypeStruct((n, width), out_dtype)
        out_spec = pl.BlockSpec((tm, width), lambda i: (i, 0))
    return pl.pallas_call(
        functools.partial(_proj_kernel, rotary_width=rotary_width),
        out_shape=out_shape,
        grid=(n // tm,),
        in_specs=in_specs,
        out_specs=out_spec,
        compiler_params=_params("parallel"),
        name=name,
    )(*args)


ONES_ROWS = 16
ATTN_SUB = 256


def _column_max(x):
    rows = x.shape[0]
    while rows > 8 and rows % 2 == 0:
        rows //= 2
        x = jnp.maximum(x[:rows], x[rows:])
    return jnp.max(x, axis=0, keepdims=True)


def _diff_attn_kernel(lam_ref, g_ref, q_ref, k_ref, v_ref, o_ref, wq_scr, vt_scr, acc_scr, m_scr,
                      sa_scr, sb_scr, ma_scr, mb_scr, *, lambda_init):
    qi = pl.program_id(2)
    heads, _, t, vd = q_ref.shape
    n_kv = k_ref.shape[2] // t
    span = 2 * t

    @pl.when(qi == 0)
    def _transpose_v():
        for hh in range(heads):
            for j in range(n_kv):
                vt_scr[hh, j, 0:vd, :] = v_ref[hh, 0, j * t:(j + 1) * t, :].astype(_F32).T.astype(_BF16)
                vt_scr[hh, j, vd:vd + ONES_ROWS, :] = jnp.ones((ONES_ROWS, t), _BF16)

    for hh in range(heads):
        q_t = (q_ref[hh, 0].astype(_F32) * (HEAD_DIM ** -0.5)).T
        first = _head0_lanes(q_t.shape, 0)
        wq_scr[:, hh * span:hh * span + t] = jnp.where(first, q_t, 0.0).astype(_BF16)
        wq_scr[:, hh * span + t:(hh + 1) * span] = jnp.where(first, 0.0, q_t).astype(_BF16)
    acc_scr[...] = jnp.zeros_like(acc_scr)
    m_scr[...] = jnp.full_like(m_scr, -jnp.inf)

    chains = [(c * ATTN_SUB // span, slice(c * ATTN_SUB, (c + 1) * ATTN_SUB))
              for c in range(heads * span // ATTN_SUB)]

    def scores(j, buf, chain):
        s_ref, max_ref = buf
        hh, cols = chain
        k_t = k_ref[hh, 0, pl.ds(pl.multiple_of(j * t, t), t), :]
        s = jnp.dot(k_t, wq_scr[:, cols], preferred_element_type=_F32)
        s_ref[:, cols] = s
        max_ref[:, cols] = _column_max(s)

    def softmax_pv(j, buf, chain, masked):
        s_ref, max_ref = buf
        hh, cols = chain
        s = s_ref[:, cols]
        if masked:
            key = lax.broadcasted_iota(jnp.int32, (t, ATTN_SUB), 0)
            qry = lax.broadcasted_iota(jnp.int32, (t, ATTN_SUB), 1) + cols.start % t
            s = jnp.where(key <= qry, s, -jnp.inf)
            m_tile = _column_max(s)
        else:
            m_tile = max_ref[:, cols]
        m_prev = m_scr[:, cols]
        m_new = jnp.maximum(m_prev, m_tile)
        alpha = jnp.exp(m_prev - m_new)
        p = jnp.exp(s - m_new).astype(_BF16)
        acc_scr[:, cols] = alpha * acc_scr[:, cols] + jnp.dot(vt_scr[hh, j], p, preferred_element_type=_F32)
        m_scr[:, cols] = m_new

    def stage(j_next, s_next, j, s_cur, masked):
        for chain in chains:
            if j_next is not None:
                scores(j_next, s_next, chain)
            softmax_pv(j, s_cur, chain, masked)

    buf_a = (sa_scr, ma_scr)
    buf_b = (sb_scr, mb_scr)
    for chain in chains:
        scores(0, buf_a, chain)

    def body(jj, carry):
        j = 2 * jj
        stage(j + 1, buf_b, j, buf_a, False)
        stage(j + 2, buf_a, j + 1, buf_b, False)
        return carry

    lax.fori_loop(0, qi // 2, body, 0)

    @pl.when(qi % 2 == 0)
    def _even_tail():
        stage(None, None, qi, buf_a, True)

    @pl.when(qi % 2 == 1)
    def _odd_tail():
        stage(qi, buf_b, qi - 1, buf_a, False)
        stage(None, None, qi, buf_b, True)

    lq = lam_ref[...]
    lam = (jnp.exp(jnp.sum(lq[0:1] * lq[1:2], axis=1, keepdims=True))
           - jnp.exp(jnp.sum(lq[2:3] * lq[3:4], axis=1, keepdims=True)) + lambda_init)
    for hh in range(heads):
        c1 = slice(hh * span, hh * span + t)
        c2 = slice(hh * span + t, (hh + 1) * span)
        o1 = acc_scr[0:vd, c1] / acc_scr[vd:vd + 1, c1]
        o2 = acc_scr[0:vd, c2] / acc_scr[vd:vd + 1, c2]
        o = (o1 - lam * o2).T
        y = o * lax.rsqrt(jnp.mean(o * o, axis=1, keepdims=True) + LN_EPS) * g_ref[...]
        o_ref[0, :, hh * vd:(hh + 1) * vd] = (y * (1.0 - lambda_init)).astype(o_ref.dtype)


HEAD_GROUP = 4


def _diff_attention(qkv, lam_qk, subln, lambda_init, tile=512):
    _, b, s, vd = qkv.shape
    t = min(tile, s)
    hg = HEAD_GROUP
    assert s % t == 0 and (2 * t) % ATTN_SUB == 0 and t % ATTN_SUB == 0 and vd == 2 * HEAD_DIM
    assert DIFF_HEADS % hg == 0
    groups = DIFF_HEADS // hg
    cols = hg * 2 * t
    return pl.pallas_call(
        functools.partial(_diff_attn_kernel, lambda_init=lambda_init),
        out_shape=jax.ShapeDtypeStruct((b, s, DIFF_WIDTH), _BF16),
        grid=(b, groups, s // t),
        in_specs=[
            pl.BlockSpec((4, HEAD_DIM), lambda bi, h, qi: (0, 0)),
            pl.BlockSpec((1, vd), lambda bi, h, qi: (0, 0)),
            pl.BlockSpec((hg, 1, t, vd), lambda bi, h, qi: (h, bi, qi, 0)),
            pl.BlockSpec((hg, 1, s, vd), lambda bi, h, qi: (groups + h, bi, 0, 0),
                         pipeline_mode=pl.Buffered(1)),
            pl.BlockSpec((hg, 1, s, vd), lambda bi, h, qi: (2 * groups + h, bi, 0, 0),
                         pipeline_mode=pl.Buffered(1)),
        ],
        out_specs=pl.BlockSpec((1, t, hg * vd), lambda bi, h, qi: (bi, qi, h)),
        scratch_shapes=[pltpu.VMEM((vd, cols), _BF16),
                        pltpu.VMEM((hg, s // t, vd + ONES_ROWS, t), _BF16),
                        pltpu.VMEM((vd + ONES_ROWS, cols), _F32),
                        pltpu.VMEM((1, cols), _F32),
                        pltpu.VMEM((t, cols), _F32),
                        pltpu.VMEM((t, cols), _F32),
                        pltpu.VMEM((1, cols), _F32),
                        pltpu.VMEM((1, cols), _F32)],
        compiler_params=_params("parallel", "parallel", "arbitrary", vmem_limit=V7X_VMEM_BYTES * 7 // 8),
        name="diff_attn",
    )(lam_qk, subln.reshape(1, vd), qkv, qkv, qkv)


HIST = 8
LRU_TILE = 128


def _lru_kernel(x_ref, wx_ref, wg_ref, cw_ref, cb_ref, wbd_ref, gb_ref, lam_ref, o_ref,
                xbuf, a_tm, u_tm, g_buf, h_ref):
    nb, t, d = x_ref.shape
    w = RNN_WIDTH
    lane_blocks = w // LANES

    @pl.when(pl.program_id(0) == 0)
    def _init():
        xbuf[:, 0:HIST, :] = jnp.zeros((nb, HIST, w), _F32)
        h_ref[...] = jnp.zeros_like(h_ref)

    x_all = x_ref[...].reshape(nb * t, d)
    xr_all = jnp.dot(x_all, wx_ref[...], preferred_element_type=_F32)
    g_buf[...] = jax.nn.gelu(jnp.dot(x_all, wg_ref[...], preferred_element_type=_F32))
    neg = -lam_ref[...]
    softplus = jnp.maximum(neg, 0.0) + jnp.log1p(jnp.exp(-jnp.abs(neg)))

    for b in range(nb):
        xbuf[b, HIST:HIST + t, :] = xr_all[b * t:(b + 1) * t]
        xc = cb_ref[...]
        for tap in range(CONV_WIDTH):
            start = HIST - (CONV_WIDTH - 1) + tap
            xc = xc + xbuf[b, start:start + t, :] * cw_ref[tap:tap + 1, :]
        xbuf[b, 0:HIST, :] = xbuf[b, t:t + HIST, :]

        xcb = xc.astype(_BF16)
        gates = []
        for g in range(2):
            parts = [jnp.dot(xcb[:, j * GATE_TILE:(j + 1) * GATE_TILE], wbd_ref[g, j],
                             preferred_element_type=_F32) for j in range(w // GATE_TILE)]
            gates.append(jnp.concatenate(parts, axis=1) + gb_ref[g:g + 1, :])
        r = jax.nn.sigmoid(gates[0])
        i = jax.nn.sigmoid(gates[1])
        log_a = -LRU_C * r * softplus
        a = jnp.exp(log_a)
        u = jnp.sqrt(-jnp.tanh(log_a) * (1.0 + a * a)) * (i * xc)
        rows = pl.ds(b, t, stride=nb)
        for lb in range(lane_blocks):
            a_tm[lb, rows, :] = a[:, lb * LANES:(lb + 1) * LANES]
            u_tm[lb, rows, :] = u[:, lb * LANES:(lb + 1) * LANES]

    def step(ts, h):
        rows = pl.ds(pl.multiple_of(ts * nb, nb), nb)
        out = []
        for lb in range(lane_blocks):
            h_lb = a_tm[lb, rows, :] * h[lb] + u_tm[lb, rows, :]
            u_tm[lb, rows, :] = h_lb
            out.append(h_lb)
        return tuple(out)

    h0 = tuple(h_ref[:, lb * LANES:(lb + 1) * LANES] for lb in range(lane_blocks))
    h_end = lax.fori_loop(0, t, step, h0, unroll=8)
    for lb in range(lane_blocks):
        h_ref[:, lb * LANES:(lb + 1) * LANES] = h_end[lb]

    for b in range(nb):
        rows = pl.ds(b, t, stride=nb)
        h_b = jnp.concatenate([u_tm[lb, rows, :] for lb in range(lane_blocks)], axis=1)
        o_ref[b] = (h_b * g_buf[b * t:(b + 1) * t, :]).astype(o_ref.dtype)


def _block_diag_gates(gate_w):
    per = GATE_TILE // LRU_BLOCK
    n_tiles = RNN_WIDTH // GATE_TILE
    gw = gate_w.reshape(2, n_tiles, per, LRU_BLOCK, LRU_BLOCK)
    eye = jnp.eye(per, dtype=gate_w.dtype)
    dense = jnp.einsum("gtpcd,pq->gtpcqd", gw, eye)
    return dense.reshape(2, n_tiles, GATE_TILE, GATE_TILE)


def _rg_lru(x16, w_rnn, conv_w, conv_b, gate_w, gate_b, lru_lambda):
    b, s, d = x16.shape
    t = min(LRU_TILE, s)
    assert s % t == 0
    w = RNN_WIDTH
    full = lambda shape: pl.BlockSpec(shape, lambda ti: (0,) * len(shape))
    return pl.pallas_call(
        _lru_kernel,
        out_shape=jax.ShapeDtypeStruct((b, s, w), _BF16),
        grid=(s // t,),
        in_specs=[pl.BlockSpec((b, t, d), lambda ti: (0, ti, 0)),
                  pl.BlockSpec((d, w), lambda ti: (0, 0)),
                  pl.BlockSpec((d, w), lambda ti: (0, 1)),
                  full((CONV_WIDTH, w)), full((1, w)),
                  full((2, w // GATE_TILE, GATE_TILE, GATE_TILE)), full((2, w)), full((1, w))],
        out_specs=pl.BlockSpec((b, t, w), lambda ti: (0, ti, 0)),
        scratch_shapes=[pltpu.VMEM((b, t + HIST, w), _F32),
                        pltpu.VMEM((w // LANES, t * b, LANES), _F32),
                        pltpu.VMEM((w // LANES, t * b, LANES), _F32),
                        pltpu.VMEM((t * b, w), _F32),
                        pltpu.VMEM((b, w), _F32)],
        compiler_params=_params("arbitrary"),
        name="rg_lru",
    )(x16, w_rnn, w_rnn, conv_w, conv_b.reshape(1, w), _block_diag_gates(gate_w).astype(_BF16), gate_b,
      lru_lambda.reshape(1, w))


DIL_TILE = DILATIONS[-1] * DIL_STEPS
DIL_UNROLL = 8


def _rows(start, size, stride):
    return pl.ds(start, size) if stride == 1 else pl.ds(start, size, stride=stride)


def _dil_kernel(q_ref, k_ref, v_ref, o_ref, qf, kf, vf, o_run, l_run):
    i = pl.program_id(1)
    t = DIL_TILE
    st = DIL_STEPS
    hd = HEAD_DIM

    pairs = DIL_HEADS // 2

    @pl.when(i == 0)
    def _no_history():
        kf[:, 0:t, :] = jnp.zeros((pairs, t, LANES), _F32)
        vf[:, 0:t, :] = jnp.zeros((pairs, t, LANES), _F32)

    @pl.when(i > 0)
    def _shift_history():
        kf[:, 0:t, :] = kf[:, t:2 * t, :]
        vf[:, 0:t, :] = vf[:, t:2 * t, :]

    for pair in range(pairs):
        lanes = slice(pair * LANES, (pair + 1) * LANES)
        qf[pair] = q_ref[0, :, lanes].astype(_F32) * (hd ** -0.5)
        kf[pair, t:2 * t, :] = k_ref[0, :, lanes].astype(_F32)
        vf[pair, t:2 * t, :] = v_ref[0, :, lanes].astype(_F32)

    key = lax.broadcasted_iota(jnp.int32, (2 * st, 2 * st), 0)
    qry = lax.broadcasted_iota(jnp.int32, (2 * st, 2 * st), 1) % st
    dist = st + qry - key
    band = (dist >= 0) & (dist <= st)
    own_block = key >= st
    first_head = _head0_lanes((LANES, st), 0)
    ones = jnp.ones((ONES_ROWS, 2 * st), _F32)

    def block(idx, dil, merge):
        phase = idx % dil
        n = idx // dil
        start = phase + dil * st * n
        q_rows = _rows(start, st, dil)
        kv_rows = _rows(t + start - dil * st, 2 * st, dil)
        valid = band & (own_block | (i > 0) | (n > 0))
        scores = []
        for pair in range(pairs):
            q_t = qf[pair, q_rows, :].T
            k_b = kf[pair, kv_rows, :].astype(_BF16)
            w = jnp.concatenate([jnp.where(first_head, q_t, 0.0),
                                 jnp.where(first_head, 0.0, q_t)], axis=1).astype(_BF16)
            scores.append(jnp.dot(k_b, w, preferred_element_type=_F32))
        o_parts, lse_rows = [], []
        for pair in range(pairs):
            v_t = vf[pair, kv_rows, :].T
            s = jnp.where(valid, scores[pair], -jnp.inf)
            m = _column_max(s)
            p = jnp.exp(s - m).astype(_BF16)
            lhs = jnp.concatenate([v_t[0:hd], ones, v_t[hd:2 * hd], ones], axis=0).astype(_BF16)
            pv = jnp.dot(lhs, p, preferred_element_type=_F32)
            for hh in range(2):
                rr = (hd + ONES_ROWS) * hh
                cc = slice(st * hh, st * (hh + 1))
                l = pv[rr + hd:rr + hd + 1, cc]
                o_parts.append(pv[rr:rr + hd, cc] / l)
                lse_rows.append(m[:, cc] + jnp.log(l))
        lse = jnp.concatenate(lse_rows, axis=0)
        if merge:
            run_l = l_run[q_rows, :].T[0:DIL_HEADS]
            mx = jnp.maximum(run_l, lse)
            w_run = jnp.exp(run_l - mx)
            w_cur = jnp.exp(lse - mx)
            den = w_run + w_cur
            a = w_run / den
            b = w_cur / den
            lse = mx + jnp.log(den)
        outs = []
        for pair in range(pairs):
            halves = o_parts[2 * pair:2 * pair + 2]
            if merge:
                run_o_t = o_run[pair, q_rows, :].T
                halves = [a[h:h + 1] * run_o_t[hh * hd:(hh + 1) * hd] + b[h:h + 1] * halves[hh]
                          for hh, h in enumerate((2 * pair, 2 * pair + 1))]
            outs.append(jnp.concatenate(halves, axis=0).T)
        pad = jnp.zeros((LANES - DIL_HEADS, st), _F32)
        return q_rows, outs, jnp.concatenate([lse, pad], axis=0).T

    def store(q_rows, outs, lse):
        for pair in range(pairs):
            o_run[pair, q_rows, :] = outs[pair]
        l_run[q_rows, :] = lse

    for idx_p, dil in enumerate(reversed(DILATIONS)):
        def body(trip, carry, dil=dil, merge=idx_p > 0):
            results = [block(DIL_UNROLL * trip + u, dil, merge) for u in range(DIL_UNROLL)]
            for res in results:
                store(*res)
            return carry
        lax.fori_loop(0, t // st // DIL_UNROLL, body, 0)

    for pair in range(pairs):
        o_ref[0, :, pair * LANES:(pair + 1) * LANES] = o_run[pair].astype(o_ref.dtype)


def _dilated_attention(qkv):
    b, s, _ = qkv.shape
    t = DIL_TILE
    assert s % t == 0
    return pl.pallas_call(
        _dil_kernel,
        out_shape=jax.ShapeDtypeStruct((b, s, DIL_WIDTH), _BF16),
        grid=(b, s // t),
        in_specs=[pl.BlockSpec((1, t, DIL_WIDTH), lambda bi, i: (bi, i, 0)),
                  pl.BlockSpec((1, t, DIL_WIDTH), lambda bi, i: (bi, i, 1)),
                  pl.BlockSpec((1, t, DIL_WIDTH), lambda bi, i: (bi, i, 2))],
        out_specs=pl.BlockSpec((1, t, DIL_WIDTH), lambda bi, i: (bi, i, 0)),
        scratch_shapes=[pltpu.VMEM((DIL_WIDTH // LANES, t, LANES), _F32),
                        pltpu.VMEM((DIL_WIDTH // LANES, 2 * t, LANES), _F32),
                        pltpu.VMEM((DIL_WIDTH // LANES, 2 * t, LANES), _F32),
                        pltpu.VMEM((DIL_WIDTH // LANES, t, LANES), _F32),
                        pltpu.VMEM((t, LANES), _F32)],
        compiler_params=_params("parallel", "arbitrary"),
        name="dil_attn",
    )(qkv, qkv, qkv)


def _merge_kernel(oa_ref, ob_ref, oc_ref, x16_ref, x_ref, wg_ref, wa_ref, wb_ref, wc_ref, wo_ref,
                  g_ref, b_ref, y32_ref, y16_ref):
    tm = x_ref.shape[0]
    halves = [slice(0, tm // 2), slice(tm // 2, tm)] if tm % 32 == 0 else [slice(0, tm)]
    branches = ((oa_ref, wa_ref), (ob_ref, wb_ref), (oc_ref, wc_ref))
    merged = []
    for rows in halves:
        x16 = x16_ref[rows, :]
        acc = None
        for idx, (o_ref, w_ref) in enumerate(branches):
            gate = jnp.dot(x16, wg_ref[idx], preferred_element_type=_F32)
            branch = jnp.dot(o_ref[rows, :], w_ref[...], preferred_element_type=_F32)
            term = jax.nn.sigmoid(gate) * branch
            acc = term if acc is None else acc + term
        merged.append(acc.astype(_BF16))
    projected = [jnp.dot(m, wo_ref[...], preferred_element_type=_F32) for m in merged]
    for rows, h in zip(halves, projected):
        y = _layer_norm(DEEPNORM_ALPHA * x_ref[rows, :] + h, g_ref[...], b_ref[...])
        y32_ref[rows, :] = y
        y16_ref[rows, :] = y.astype(_BF16)


def _merge(o_a, o_b, o_c, x16, x, w_gates, w_a, w_b, w_c, w_o, g, b, tm=512):
    n, d = x.shape
    tm = min(tm, n)
    assert n % tm == 0
    row = lambda width: pl.BlockSpec((tm, width), lambda i: (i, 0))
    full = lambda shape: pl.BlockSpec(shape, lambda i: (0,) * len(shape), pipeline_mode=pl.Buffered(1))
    return pl.pallas_call(
        _merge_kernel,
        out_shape=(jax.ShapeDtypeStruct((n, d), _F32), jax.ShapeDtypeStruct((n, d), _BF16)),
        grid=(n // tm,),
        in_specs=[row(DIFF_WIDTH), row(RNN_WIDTH), row(DIL_WIDTH), row(d), row(d),
                  full((3, d, d)), full((DIFF_WIDTH, d)), full((RNN_WIDTH, d)), full((DIL_WIDTH, d)),
                  full((d, d)), full((1, d)), full((1, d))],
        out_specs=(row(d), row(d)),
        compiler_params=_params("parallel"),
        name="merge_ln1",
    )(o_a, o_b, o_c, x16, x, w_gates, w_a, w_b, w_c, w_o, g.reshape(1, d), b.reshape(1, d))


def _xattn_kernel(x32_ref, x16_ref, kv_ref, wq_ref, wo_ref, g_ref, b_ref, y32_ref, y16_ref):
    d = D_MODEL
    hd = XATTN_HEAD_DIM
    tm = x16_ref.shape[1]
    halves = [slice(0, tm // 2), slice(tm // 2, tm)] if tm % 16 == 0 else [slice(0, tm)]
    qs = []
    for rows in halves:
        q = jnp.dot(x16_ref[0, rows, :], wq_ref[...], preferred_element_type=_F32)
        qs.append((q * (hd ** -0.5)).astype(_BF16))
    scores = [[lax.dot_general(q[:, h * hd:(h + 1) * hd], kv_ref[0, :, h * hd:(h + 1) * hd], _NT,
                               preferred_element_type=_F32) for h in range(XATTN_HEADS)] for q in qs]
    attended = []
    for half_scores in scores:
        heads = []
        for h, s in enumerate(half_scores):
            v_h = kv_ref[0, :, d + h * hd:d + (h + 1) * hd]
            p = jnp.exp(s - jnp.max(s, axis=1, keepdims=True))
            l = jnp.sum(p, axis=1, keepdims=True)
            heads.append(jnp.dot(p.astype(_BF16), v_h, preferred_element_type=_F32) / l)
        attended.append(jnp.concatenate(heads, axis=1).astype(_BF16))
    projected = [jnp.dot(o, wo_ref[...], preferred_element_type=_F32) for o in attended]
    for rows, h_out in zip(halves, projected):
        y = _layer_norm(DEEPNORM_ALPHA * x32_ref[0, rows, :] + h_out, g_ref[...], b_ref[...])
        y32_ref[0, rows, :] = y
        y16_ref[0, rows, :] = y.astype(_BF16)


def _cross_attention(x32, x16, kv, wq, wo, g, b, tm=512):
    bsz, s, d = x32.shape
    tm = min(tm, s)
    assert s % tm == 0
    row = pl.BlockSpec((1, tm, d), lambda bi, i: (bi, i, 0))
    full = lambda shape: pl.BlockSpec(shape, lambda bi, i: (0, 0))
    return pl.pallas_call(
        _xattn_kernel,
        out_shape=(jax.ShapeDtypeStruct((bsz, s, d), _F32), jax.ShapeDtypeStruct((bsz, s, d), _BF16)),
        grid=(bsz, s // tm),
        in_specs=[row, row, pl.BlockSpec((1, MEM_TOKENS, 2 * d), lambda bi, i: (bi, 0, 0)),
                  full((d, d)), full((d, d)), full((1, d)), full((1, d))],
        out_specs=(row, row),
        compiler_params=_params("parallel", "parallel"),
        name="xattn_ln2",
    )(x32, x16, kv, wq, wo, g.reshape(1, d), b.reshape(1, d))


def _ffn_kernel(x32_ref, x16_ref, wg_ref, wu_ref, wd_ref, g_ref, b_ref, y32_ref, y16_ref, acc_ref):
    c = pl.program_id(1)

    @pl.when(c == 0)
    def _init():
        acc_ref[...] = jnp.zeros_like(acc_ref)

    x = x16_ref[...]
    gate = jnp.dot(x, wg_ref[...], preferred_element_type=_F32)
    up = jnp.dot(x, wu_ref[...], preferred_element_type=_F32)
    hidden = (jax.nn.silu(gate) * up).astype(_BF16)
    acc_ref[...] += jnp.dot(hidden, wd_ref[...], preferred_element_type=_F32)

    @pl.when(c == pl.num_programs(1) - 1)
    def _fin():
        y = _layer_norm(DEEPNORM_ALPHA * x32_ref[...] + acc_ref[...], g_ref[...], b_ref[...])
        y32_ref[...] = y
        y16_ref[...] = y.astype(_BF16)


def _ffn(x32, x16, w_up, w_down, g, b, tm=512, chunks=2):
    n, d = x32.shape
    tm = min(tm, n)
    th = FFN_HIDDEN // chunks
    assert n % tm == 0 and FFN_HIDDEN % chunks == 0 and th % LANES == 0
    row = pl.BlockSpec((tm, d), lambda i, c: (i, 0))
    full = pl.BlockSpec((1, d), lambda i, c: (0, 0))
    return pl.pallas_call(
        _ffn_kernel,
        out_shape=(jax.ShapeDtypeStruct((n, d), _F32), jax.ShapeDtypeStruct((n, d), _BF16)),
        grid=(n // tm, chunks),
        in_specs=[row, row,
                  pl.BlockSpec((d, th), lambda i, c: (0, c)),
                  pl.BlockSpec((d, th), lambda i, c: (0, chunks + c)),
                  pl.BlockSpec((th, d), lambda i, c: (c, 0)),
                  full, full],
        out_specs=(row, row),
        scratch_shapes=[pltpu.VMEM((tm, d), _F32)],
        compiler_params=_params("parallel", "arbitrary"),
        name="ffn_ln3",
    )(x32, x16, w_up, w_up, w_down, g.reshape(1, d), b.reshape(1, d))


def kernel(x, mem, w_in, lam_qk, diff_subln, conv_w, conv_b, gate_w, gate_b, lru_lambda, w_br_a, w_br_b, w_br_c, w_out, ln1_g, ln1_b, xq, xkv, xo, ln2_g, ln2_b, w_up, w_down, ln3_g, ln3_b):
    bsz, seq, d = x.shape
    n = bsz * seq
    tables = _rotary_tables(seq)
    mem16 = mem.reshape(bsz * mem.shape[1], d).astype(_BF16)
    x32 = x.reshape(n, d)
    x16 = _to_bf16(x32)
    for l in range(DEPTH):
        w = w_in[l].astype(_BF16)
        lambda_init = 0.8 - 0.6 * math.exp(-0.3 * l)
        w_qkv_a, w_qkv_c = (
            jnp.concatenate([_permute_qk_columns(w[:, c0:c0 + 1024]), w[:, c0 + 1024:c0 + 1536]], axis=1)
            for c0 in (COL_QK_A, COL_QK_C))
        qkv_a = _project("proj_qkv_a", x16, w_qkv_a, _BF16, seq, tables, rotary_width=1024, slabs=True)
        qkv_c = _project("proj_qkv_c", x16, w_qkv_c, _BF16, seq, tables, rotary_width=1024)
        w_gates = w[:, COL_GATE:COL_GATE + 3 * d].reshape(d, 3, d).transpose(1, 0, 2)

        o_a = _diff_attention(qkv_a.reshape(-1, bsz, seq, LANES), lam_qk[l], diff_subln[l], lambda_init)
        o_b = _rg_lru(x16.reshape(bsz, seq, d), w[:, COL_RNN:COL_RNN + 2 * RNN_WIDTH], conv_w[l], conv_b[l],
                      gate_w[l], gate_b[l], lru_lambda[l])
        o_c = _dilated_attention(qkv_c.reshape(bsz, seq, 1536))

        x32, x16 = _merge(o_a.reshape(n, 512), o_b.reshape(n, 1024), o_c.reshape(n, 512), x16, x32,
                          w_gates, w_br_a[l].astype(_BF16), w_br_b[l].astype(_BF16), w_br_c[l].astype(_BF16),
                          w_out[l].astype(_BF16), ln1_g[l], ln1_b[l])

        kv = _project("proj_mem_kv", mem16, xkv[l].astype(_BF16), _BF16, mem.shape[1])
        x32, x16 = _cross_attention(x32.reshape(bsz, seq, d), x16.reshape(bsz, seq, d),
                                    kv.reshape(bsz, mem.shape[1], 2 * d), xq[l].astype(_BF16),
                                    xo[l].astype(_BF16), ln2_g[l], ln2_b[l])
        x32, x16 = _ffn(x32.reshape(n, d), x16.reshape(n, d), w_up[l].astype(_BF16),
                        w_down[l].astype(_BF16), ln3_g[l], ln3_b[l])
    return x32.reshape(bsz, seq, d)
```
